```python
import math
import jax, jax.numpy as jnp
from jax import lax
import numpy as np

D_MODEL = 1024
BATCH = 4
SEQ = 4096
DEPTH = 4
DEC_BATCH = 128
DEC_SEQ = 4
PAST_LEN = 2048
PAGE_SIZE = 128

N_MIXERS = 2
N_RWKV = (DEPTH + 1) // 2
N_NSA = DEPTH // 2

RW_HEAD = 64
RW_HEADS = D_MODEL // RW_HEAD
D_DECAY_LORA = 64
D_AAA_LORA = 64
D_MV_LORA = 32
D_GATE_LORA = 160
GN_EPS = 64e-5

NSA_HEADS = 16
NSA_KV_HEADS = 4
NSA_HD = D_MODEL // NSA_HEADS
NSA_GROUP = NSA_HEADS // NSA_KV_HEADS
NSA_QDIM = NSA_HEADS * NSA_HD
NSA_KVDIM = NSA_KV_HEADS * NSA_HD
NSA_IN_DIM = 2 * NSA_QDIM + 6 * NSA_KVDIM + 3 * NSA_HEADS
CMP_BLOCK = 32
CMP_STRIDE = 16
CMP_HIDDEN = NSA_HD
SEL_BLOCK = 64
SEL_TOPN = 16
WINDOW = 512
Q_BLOCK = 128

NUM_BUCKETS = 32
MAX_DISTANCE = 1024
RMS_EPS = 1e-6

kernel_name = 'rwkv7_nsa_hybrid_step'


def rms_norm(x, w, eps=RMS_EPS):
    xf = x.astype(jnp.float32)
    y = xf * lax.rsqrt(jnp.mean(xf * xf, axis=-1, keepdims=True) + eps)
    return (y * w.astype(jnp.float32)).astype(x.dtype)


def masked_softmax(logits, valid, axes):
    lf = jnp.where(valid, logits.astype(jnp.float32), -jnp.inf)
    m = jnp.max(lf, axis=axes, keepdims=True)
    m = jnp.where(jnp.isfinite(m), m, 0.0)
    e = jnp.exp(lf - m)
    return e / jnp.maximum(jnp.sum(e, axis=axes, keepdims=True), 1e-30)


def rel_bucket(dist):
    n = jnp.maximum(dist, 0)
    max_exact = NUM_BUCKETS // 2
    nf = jnp.maximum(n, 1).astype(jnp.float32)
    large = max_exact + (jnp.log(nf / max_exact) / math.log(MAX_DISTANCE / max_exact) * (NUM_BUCKETS - max_exact)).astype(jnp.int32)
    large = jnp.minimum(large, NUM_BUCKETS - 1)
    return jnp.where(n < max_exact, n, large)


def cmp_sel_overlap(n_cmp, n_sel):
    cs = jnp.arange(n_cmp)[:, None] * CMP_STRIDE
    ss = jnp.arange(n_sel)[None, :] * SEL_BLOCK
    ov = jnp.maximum(jnp.minimum(cs + CMP_BLOCK, ss + SEL_BLOCK) - jnp.maximum(cs, ss), 0)
    return ov.astype(jnp.float32) / CMP_BLOCK


def compress(rows, pos_emb, w1, w2):
    L = rows.shape[1]
    n_cmp = (L - CMP_BLOCK) // CMP_STRIDE + 1
    idx = jnp.arange(n_cmp)[:, None] * CMP_STRIDE + jnp.arange(CMP_BLOCK)[None, :]
    blocks = rows[:, idx] + pos_emb[None, None, :, None, :]
    h = jax.nn.silu(jnp.einsum('bnjkd,jde->bnke', blocks, w1.reshape(CMP_BLOCK, NSA_HD, CMP_HIDDEN)))
    return jnp.einsum('bnke,ed->bnkd', h, w2)


def nsa_compressed(kc_rows, vc_rows, k_norm, cmp_pos, cmp_w1, cmp_w2):
    k_cmp = rms_norm(compress(kc_rows, cmp_pos[0], cmp_w1[0], cmp_w2[0]), k_norm[2])
    v_cmp = compress(vc_rows, cmp_pos[1], cmp_w1[1], cmp_w2[1])
    cmp_end = jnp.arange(k_cmp.shape[1]) * CMP_STRIDE + (CMP_BLOCK - 1)
    return k_cmp, v_cmp, cmp_end


def nsa_project(h, w_in, q_norm, k_norm):
    B, T, _ = h.shape
    cuts = [NSA_QDIM + i * NSA_KVDIM for i in range(7)] + [2 * NSA_QDIM + 6 * NSA_KVDIM]
    q, kc, vc, ks, vs, kw, vw, z, gl = jnp.split(h @ w_in, cuts, axis=-1)
    q = rms_norm(q.reshape(B, T, NSA_HEADS, NSA_HD), q_norm)
    kc, vc, vs, vw = (t.reshape(B, T, NSA_KV_HEADS, NSA_HD) for t in (kc, vc, vs, vw))
    ks = rms_norm(ks.reshape(B, T, NSA_KV_HEADS, NSA_HD), k_norm[0])
    kw = rms_norm(kw.reshape(B, T, NSA_KV_HEADS, NSA_HD), k_norm[1])
    gates = jax.nn.sigmoid(gl).reshape(B, T, NSA_HEADS, 3)
    return q, kc, vc, ks, vs, kw, vw, z, gates


def nsa_attend(q, gates, qpos, k_cmp, v_cmp, cmp_end, k_sel, v_sel, k_win, v_win, win_pos, rel_table):
    tq = q.shape[0]
    scale = NSA_HD ** -0.5
    qg = q.reshape(tq, NSA_KV_HEADS, NSA_GROUP, NSA_HD)
    tg = rel_table.reshape(NUM_BUCKETS, NSA_KV_HEADS, NSA_GROUP)
    d_c = qpos[:, None] - cmp_end[None, :]
    b_c = jnp.transpose(tg[rel_bucket(d_c)], (0, 2, 3, 1))
    l_c = jnp.einsum('tkgd,nkd->tkgn', qg, k_cmp) * scale + b_c
    p_c = masked_softmax(l_c, (d_c >= 0)[:, None, None, :], -1)
    o_c = jnp.einsum('tkgn,nkd->tkgd', p_c.astype(v_cmp.dtype), v_cmp)
    n_sel = k_sel.shape[0]
    imp = jnp.einsum('tkgn,ns->tks', p_c, cmp_sel_overlap(k_cmp.shape[0], n_sel))
    blk = jnp.arange(n_sel)
    cur = (qpos // SEL_BLOCK)[:, None, None]
    forced = (blk == 0) | (blk == cur) | (blk == cur - 1)
    future = (blk * SEL_BLOCK)[None, None, :] > qpos[:, None, None]
    score = jnp.where(future, -jnp.inf, jnp.where(forced, jnp.inf, imp))
    _, idx = lax.top_k(score, min(SEL_TOPN, n_sel))
    kv_ix = jnp.arange(NSA_KV_HEADS)[None, :, None]
    kg = jnp.transpose(k_sel, (2, 0, 1, 3))[kv_ix, idx]
    vg = jnp.transpose(v_sel, (2, 0, 1, 3))[kv_ix, idx]
    pos_s = idx[..., None] * SEL_BLOCK + jnp.arange(SEL_BLOCK)
    d_s = qpos[:, None, None, None] - pos_s
    b_s = jnp.moveaxis(tg[rel_bucket(d_s), kv_ix[..., None]], -1, 2)
    l_s = jnp.einsum('tkgd,tkjsd->tkgjs', qg, kg) * scale + b_s
    p_s = masked_softmax(l_s, (d_s >= 0)[:, :, None], (-2, -1))
    o_s = jnp.einsum('tkgjs,tkjsd->tkgd', p_s.astype(vg.dtype), vg)
    d_w = qpos[:, None] - win_pos[None, :]
    valid_w = (d_w >= 0) & (d_w < WINDOW) & (win_pos >= 0)[None, :]
    b_w = jnp.transpose(tg[rel_bucket(d_w)], (0, 2, 3, 1))
    l_w = jnp.einsum('tkgd,lkd->tkgl', qg, k_win) * scale + b_w
    p_w = masked_softmax(l_w, valid_w[:, None, None, :], -1)
    o_w = jnp.einsum('tkgl,lkd->tkgd', p_w.astype(v_win.dtype), v_win)
    g = gates.reshape(tq, NSA_KV_HEADS, NSA_GROUP, 3).astype(o_c.dtype)
    o = g[..., 0:1] * o_c + g[..., 1:2] * o_s + g[..., 2:3] * o_w
    return o.reshape(tq, NSA_HEADS, NSA_HD)


def nsa_out(o, z, w_o):
    B, T = o.shape[:2]
    return (o.reshape(B, T, NSA_QDIM).astype(z.dtype) * jax.nn.silu(z)) @ w_o


def nsa_layer_prompt(h, w_in, q_norm, k_norm, cmp_pos, cmp_w1, cmp_w2, w_o, rel_table):
    B, T, _ = h.shape
    q, kc, vc, ks, vs, kw, vw, z, gates = nsa_project(h, w_in, q_norm, k_norm)
    k_cmp, v_cmp, cmp_end = nsa_compressed(kc, vc, k_norm, cmp_pos, cmp_w1, cmp_w2)
    n_sel = T // SEL_BLOCK
    ks_b = ks.reshape(B, n_sel, SEL_BLOCK, NSA_KV_HEADS, NSA_HD)
    vs_b = vs.reshape(B, n_sel, SEL_BLOCK, NSA_KV_HEADS, NSA_HD)
    pad = jnp.zeros((B, WINDOW, NSA_KV_HEADS, NSA_HD), kw.dtype)
    kw_p = jnp.concatenate([pad, kw], axis=1)
    vw_p = jnp.concatenate([pad, vw], axis=1)
    n_qb = T // Q_BLOCK
    q_items = q.reshape(B * n_qb, Q_BLOCK, NSA_HEADS, NSA_HD)
    g_items = gates.reshape(B * n_qb, Q_BLOCK, NSA_HEADS, 3)
    items = jnp.arange(B * n_qb)

    def body(args):
        qi, gi, it = args
        b = it // n_qb
        s = (it % n_qb) * Q_BLOCK
        qpos = s + jnp.arange(Q_BLOCK)
        band = (1, WINDOW + Q_BLOCK, NSA_KV_HEADS, NSA_HD)
        kwb = lax.dynamic_slice(kw_p, (b, s, 0, 0), band)[0]
        vwb = lax.dynamic_slice(vw_p, (b, s, 0, 0), band)[0]
        wpos = s - WINDOW + jnp.arange(WINDOW + Q_BLOCK)
        return nsa_attend(qi, gi, qpos, k_cmp[b], v_cmp[b], cmp_end, ks_b[b], vs_b[b], kwb, vwb, wpos, rel_table)

    o = lax.map(body, (q_items, g_items, items)).reshape(B, T, NSA_HEADS, NSA_HD)
    y = nsa_out(o, z, w_o)
    wb = min(WINDOW, T)
    return y, (kc, vc, ks, vs, kw[:, T - wb:], vw[:, T - wb:])


def nsa_layer_sample(h, c_ck, c_cv, c_sk, c_sv, win_k, win_v, page_table, w_in, q_norm, k_norm, cmp_pos, cmp_w1, cmp_w2, w_o, rel_table):
    B, T, _ = h.shape
    past = page_table.shape[1] * PAGE_SIZE
    q, kc, vc, ks, vs, kw, vw, z, gates = nsa_project(h, w_in, q_norm, k_norm)

    def with_past(cache, new):
        old = cache[page_table].reshape(B, past, NSA_KV_HEADS, NSA_HD).astype(new.dtype)
        return jnp.concatenate([old, new], axis=1)

    kc_all, vc_all, ks_all, vs_all = with_past(c_ck, kc), with_past(c_cv, vc), with_past(c_sk, ks), with_past(c_sv, vs)
    k_cmp, v_cmp, cmp_end = nsa_compressed(kc_all, vc_all, k_norm, cmp_pos, cmp_w1, cmp_w2)
    L = past + T
    n_sel = -(-L // SEL_BLOCK)
    padw = ((0, 0), (0, n_sel * SEL_BLOCK - L), (0, 0), (0, 0))
    ks_b = jnp.pad(ks_all, padw).reshape(B, n_sel, SEL_BLOCK, NSA_KV_HEADS, NSA_HD)
    vs_b = jnp.pad(vs_all, padw).reshape(B, n_sel, SEL_BLOCK, NSA_KV_HEADS, NSA_HD)
    wb = win_k.shape[1]
    kw_all = jnp.concatenate([win_k.astype(kw.dtype), kw], axis=1)
    vw_all = jnp.concatenate([win_v.astype(vw.dtype), vw], axis=1)
    wpos = past - wb + jnp.arange(wb + T)
    qpos = past + jnp.arange(T)

    def body(a):
        qi, gi, kcb, vcb, ksb, vsb, kwb, vwb = a
        return nsa_attend(qi, gi, qpos, kcb, vcb, cmp_end, ksb, vsb, kwb, vwb, wpos, rel_table)

    o = lax.map(body, (q, gates, k_cmp, v_cmp, ks_b, vs_b, kw_all, vw_all))
    y = nsa_out(o, z, w_o)
    return y, (kc, vc, ks, vs, kw_all[:, -wb:], vw_all[:, -wb:])


def wkv_scan(S0, r, decay, k, v, a_vec, b_vec):
    def step(S, inp):
        r_t, w_t, k_t, v_t, a_t, b_t = inp
        sa = jnp.einsum('bhij,bhj->bhi', S, a_t)
        S = S * w_t[:, :, None, :] + sa[..., None] * b_t[:, :, None, :] + v_t[..., None] * k_t[:, :, None, :]
        return S, jnp.einsum('bhij,bhj->bhi', S, r_t)
    xs = tuple(jnp.moveaxis(t.astype(jnp.float32), 1, 0) for t in (r, decay, k, v, a_vec, b_vec))
    S, ys = lax.scan(step, S0.astype(jnp.float32), xs)
    return jnp.moveaxis(ys, 0, 1), S


def rwkv_layer(h, shift, S0, v_first, mu, w_rkvz, w0, w1, w2, a0, a1, a2, g1, g2, k_k, k_a, r_k, ln_w, ln_b, w_o, vres):
    B, T, D = h.shape
    prev = jnp.concatenate([shift[:, None, :].astype(h.dtype), h[:, :-1]], axis=1)
    xx = prev - h
    xr, xw, xk, xv, xa, xg = [h + xx * mu[i] for i in range(6)]
    r, k, v, z = jnp.einsum('pbtc,pcd->pbtd', jnp.stack([xr, xk, xv, xg]), w_rkvz)
    w = -jax.nn.softplus(-(w0 + jnp.tanh(xw @ w1) @ w2)) - 0.5
    decay = jnp.exp(-jnp.exp(w.astype(jnp.float32)))
    if vres is None:
        v_first = v
    else:
        v0, v1, v2 = vres
        v = v + (v_first - v) * jax.nn.sigmoid(v0 + (xv @ v1) @ v2)
    a = jax.nn.sigmoid(a0 + (xa @ a1) @ a2)
    g = jax.nn.sigmoid(xg @ g1) @ g2
    heads = lambda t: t.reshape(B, T, RW_HEADS, RW_HEAD)
    kk = heads(k * k_k).astype(jnp.float32)
    kk = kk / jnp.maximum(jnp.sqrt(jnp.sum(kk * kk, axis=-1, keepdims=True)), 1e-12)
    k = k * (1 + (a - 1) * k_a)
    rh, kh, vh, ah = (heads(t).astype(jnp.float32) for t in (r, k, v, a))
    y, S = wkv_scan(S0, rh, heads(decay), kh, vh, -kk, kk * ah)
    mean = jnp.mean(y, axis=-1, keepdims=True)
    var = jnp.mean(jnp.square(y - mean), axis=-1, keepdims=True)
    yn = ((y - mean) * lax.rsqrt(var + GN_EPS)).reshape(B, T, D) * ln_w + ln_b
    bonus = (jnp.sum(rh * kh * r_k, axis=-1, keepdims=True) * vh).reshape(B, T, D)
    o = (yn + bonus) * g * jax.nn.silu(z)
    return o.astype(h.dtype) @ w_o, h[:, -1], S, v_first


def setup_inputs(seed: int = 0) -> dict:
    key = jax.random.key(seed)
    ks = iter(jax.random.split(key, 64))
    nrm = lambda shape, s=1.0: jax.random.normal(next(ks), shape, jnp.float32) * s
    uni = lambda shape, lo, hi: jax.random.uniform(next(ks), shape, jnp.float32, lo, hi)
    D = D_MODEL
    n_pages = PAST_LEN // PAGE_SIZE
    n_pool = (DEC_BATCH * n_pages * 5) // 4
    win_buf = min(WINDOW, PAST_LEN)
    page_shape = (N_NSA, n_pool, PAGE_SIZE, NSA_KV_HEADS, NSA_HD)
    win_shape = (N_NSA, DEC_BATCH, win_buf, NSA_KV_HEADS, NSA_HD)
    perm = jax.random.permutation(next(ks), n_pool)
    page_table = perm[:DEC_BATCH * n_pages].reshape(DEC_BATCH, n_pages).astype(jnp.int32)
    nv = max(N_RWKV - 1, 0)
    return {
        'x_prompt': nrm((BATCH, SEQ, D)),
        'x_sample': nrm((DEC_BATCH, DEC_SEQ, D)),
        'cache_cmp_k': nrm(page_shape),
        'cache_cmp_v': nrm(page_shape),
        'cache_sel_k': nrm(page_shape),
        'cache_sel_v': nrm(page_shape),
        'state_win_k': nrm(win_shape),
        'state_win_v': nrm(win_shape),
        'state_wkv': nrm((N_RWKV, DEC_BATCH, RW_HEADS, RW_HEAD, RW_HEAD), 0.5),
        'state_shift': nrm((N_RWKV, DEC_BATCH, D)),
        'page_table': page_table,
        'norm_w': 1.0 + nrm((DEPTH, D), 0.02),
        'rel_bias': nrm((NUM_BUCKETS, NSA_HEADS), 0.2),
        'rw_mu': uni((N_RWKV, 6, D), 0.0, 1.0),
        'rw_w_rkvz': nrm((N_RWKV, 4, D, D), D ** -0.5),
        'rw_w0': uni((N_RWKV, D), -3.0, 1.0),
        'rw_w1': nrm((N_RWKV, D, D_DECAY_LORA), D ** -0.5),
        'rw_w2': nrm((N_RWKV, D_DECAY_LORA, D), 0.5 * D_DECAY_LORA ** -0.5),
        'rw_a0': nrm((N_RWKV, D), 0.1),
        'rw_a1': nrm((N_RWKV, D, D_AAA_LORA), D ** -0.5),
        'rw_a2': nrm((N_RWKV, D_AAA_LORA, D), 0.5 * D_AAA_LORA ** -0.5),
        'rw_v0': nrm((nv, D), 0.1),
        'rw_v1': nrm((nv, D, D_MV_LORA), D ** -0.5),
        'rw_v2': nrm((nv, D_MV_LORA, D), 0.5 * D_MV_LORA ** -0.5),
        'rw_g1': nrm((N_RWKV, D, D_GATE_LORA), D ** -0.5),
        'rw_g2': nrm((N_RWKV, D_GATE_LORA, D), D_GATE_LORA ** -0.5),
        'rw_k_k': 0.85 + nrm((N_RWKV, D), 0.02),
        'rw_k_a': 1.0 + nrm((N_RWKV, D), 0.02),
        'rw_r_k': nrm((N_RWKV, RW_HEADS, RW_HEAD), 0.1),
        'rw_ln_w': 1.0 + nrm((N_RWKV, D), 0.02),
        'rw_ln_b': nrm((N_RWKV, D), 0.02),
        'rw_w_o': nrm((N_RWKV, D, D), D ** -0.5),
        'nsa_w_in': nrm((N_NSA, D, NSA_IN_DIM), D ** -0.5),
        'nsa_q_norm': 1.0 + nrm((N_NSA, NSA_HD), 0.02),
        'nsa_k_norm': 1.0 + nrm((N_NSA, 3, NSA_HD), 0.02),
        'nsa_cmp_pos': nrm((N_NSA, 2, CMP_BLOCK, NSA_HD), 0.2),
        'nsa_cmp_w1': nrm((N_NSA, 2, CMP_BLOCK * NSA_HD, CMP_HIDDEN), (CMP_BLOCK * NSA_HD) ** -0.5),
        'nsa_cmp_w2': nrm((N_NSA, 2, CMP_HIDDEN, NSA_HD), CMP_HIDDEN ** -0.5),
        'nsa_w_o': nrm((N_NSA, NSA_QDIM, D), NSA_QDIM ** -0.5),
    }


def reference(x_prompt, x_sample, cache_cmp_k, cache_cmp_v, cache_sel_k, cache_sel_v, state_win_k, state_win_v, state_wkv, state_shift, page_table,
              norm_w, rel_bias, rw_mu, rw_w_rkvz, rw_w0, rw_w1, rw_w2, rw_a0, rw_a1, rw_a2, rw_v0, rw_v1, rw_v2, rw_g1, rw_g2, rw_k_k, rw_k_a, rw_r_k,
              rw_ln_w, rw_ln_b, rw_w_o, nsa_w_in, nsa_q_norm, nsa_k_norm, nsa_cmp_pos, nsa_cmp_w1, nsa_cmp_w2, nsa_w_o):
    xp, xs = x_prompt, x_sample
    vf_p = vf_s = None
    p_wkv, p_shift, s_wkv, s_shift = [], [], [], []
    p_kv = [[] for _ in range(6)]
    s_kv = [[] for _ in range(6)]
    for i in range(DEPTH):
        j = i // N_MIXERS
        hp = rms_norm(xp, norm_w[i])
        hs = rms_norm(xs, norm_w[i])
        if i % N_MIXERS == 0:
            lw = (rw_mu[j], rw_w_rkvz[j], rw_w0[j], rw_w1[j], rw_w2[j], rw_a0[j], rw_a1[j], rw_a2[j], rw_g1[j], rw_g2[j],
                  rw_k_k[j], rw_k_a[j], rw_r_k[j], rw_ln_w[j], rw_ln_b[j], rw_w_o[j])
            vres = None if j == 0 else (rw_v0[j - 1], rw_v1[j - 1], rw_v2[j - 1])
            bp = hp.shape[0]
            yp, shp, Sp, vf_p = rwkv_layer(hp, jnp.zeros((bp, D_MODEL), hp.dtype), jnp.zeros((bp, RW_HEADS, RW_HEAD, RW_HEAD), jnp.float32), vf_p, *lw, vres)
            ys, shs, Ss, vf_s = rwkv_layer(hs, state_shift[j], state_wkv[j], vf_s, *lw, vres)
            p_wkv.append(Sp); p_shift.append(shp); s_wkv.append(Ss); s_shift.append(shs)
        else:
            nl = (nsa_w_in[j], nsa_q_norm[j], nsa_k_norm[j], nsa_cmp_pos[j], nsa_cmp_w1[j], nsa_cmp_w2[j], nsa_w_o[j])
            yp, newp = nsa_layer_prompt(hp, *nl, rel_bias)
            ys, news = nsa_layer_sample(hs, cache_cmp_k[j], cache_cmp_v[j], cache_sel_k[j], cache_sel_v[j], state_win_k[j], state_win_v[j], page_table, *nl, rel_bias)
            for m in range(6):
                p_kv[m].append(newp[m])
                s_kv[m].append(news[m])
        xp = xp + yp.astype(xp.dtype)
        xs = xs + ys.astype(xs.dtype)
    prompt_wkv, prompt_shift = jnp.stack(p_wkv), jnp.stack(p_shift)
    sample_wkv, sample_shift = jnp.stack(s_wkv), jnp.stack(s_shift)
    prompt_cmp_k, prompt_cmp_v, prompt_sel_k, prompt_sel_v, prompt_win_k, prompt_win_v = [jnp.stack(t) for t in p_kv]
    sample_cmp_k, sample_cmp_v, sample_sel_k, sample_sel_v, sample_win_k, sample_win_v = [jnp.stack(t) for t in s_kv]
    return (xp, xs, prompt_wkv, prompt_shift, prompt_cmp_k, prompt_cmp_v, prompt_sel_k, prompt_sel_v, prompt_win_k, prompt_win_v,
            sample_wkv, sample_shift, sample_cmp_k, sample_cmp_v, sample_sel_k, sample_sel_v, sample_win_k, sample_win_v)
```

```python
import functools
import math

import numpy as np
import jax
import jax.numpy as jnp
from jax import lax
from jax.experimental import pallas as pl
from jax.experimental.pallas import tpu as pltpu

F32 = jnp.float32
BF16 = jnp.bfloat16

D_MODEL = 1024
RW_HEAD = 64
RW_HEADS = 16
RW_PAIRS = 8
GN_EPS = 64e-5
RMS_EPS = 1e-6

NSA_HEADS = 16
NSA_KV = 4
NSA_GROUP = 4
NSA_HD = 64
KV_DIM = 256
CMP_BLOCK = 32
CMP_STRIDE = 16
SEL_BLOCK = 64
SEL_TOPN = 16
WINDOW = 512
NUM_BUCKETS = 32
MAX_DISTANCE = 1024
PAGE = 128

LANE = 128
VMEM_LIMIT = 48 * 1024 * 1024

NN = ((1,), (0,))
NT = ((1,), (1,))
TN = ((0,), (0,))
NEG_INF = float("-inf")


def _cp(*sem):
    return pltpu.CompilerParams(dimension_semantics=sem, vmem_limit_bytes=VMEM_LIMIT)


def _dg(a, b, dims=NN):
    return lax.dot_general(a, b, (dims, ((), ())), preferred_element_type=F32)


def _split2(x):
    hi = x.astype(BF16)
    lo = (x - hi.astype(F32)).astype(BF16)
    return hi, lo


def _split3(x):
    x0 = x.astype(BF16)
    r1 = x - x0.astype(F32)
    x1 = r1.astype(BF16)
    x2 = (r1 - x1.astype(F32)).astype(BF16)
    return x0, x1, x2


def _dot3s(a, b, dims=NN):
    return _dg(a[0], b[0], dims) + (_dg(a[0], b[1], dims) + _dg(a[1], b[0], dims))


def _dot3(a, b, dims=NN):
    return _dot3s(_split2(a), _split2(b), dims)


def _dotx(a, e, dims=NN):
    a0, a1, a2 = _split3(a)
    return _dg(a0, e, dims) + (_dg(a1, e, dims) + _dg(a2, e, dims))


def _xdot(e, b, dims=NN):
    b0, b1, b2 = _split3(b)
    return _dg(e, b0, dims) + (_dg(e, b1, dims) + _dg(e, b2, dims))


def _sigmoid(x):
    return 1.0 / (1.0 + jnp.exp(-x))


def _softplus(x):
    return jnp.maximum(x, 0.0) + jnp.log(1.0 + jnp.exp(-jnp.abs(x)))


def _rms_rows(x, w_row):
    ms = jnp.mean(x * x, axis=-1, keepdims=True)
    return x * lax.rsqrt(ms + RMS_EPS) * w_row


def _seg_mats(width):
    r = lax.broadcasted_iota(jnp.int32, (width, LANE), 0)
    c = lax.broadcasted_iota(jnp.int32, (width, LANE), 1)
    g = jnp.where(lax.shift_right_logical(r, 6) == c, 1.0, 0.0).astype(BF16)
    r2 = lax.broadcasted_iota(jnp.int32, (LANE, width), 0)
    c2 = lax.broadcasted_iota(jnp.int32, (LANE, width), 1)
    e = jnp.where(lax.shift_right_logical(c2, 6) == r2, 1.0, 0.0).astype(BF16)
    return g, e


def _segsum(x, g, e):
    return _dotx(_dotx(x, g), e)


def _masked_softmax_rows(logits, valid):
    lf = jnp.where(valid, logits, NEG_INF)
    m = jnp.max(lf, axis=-1, keepdims=True)
    m = jnp.where(m == NEG_INF, 0.0, m)
    ex = jnp.exp(lf - m)
    return ex / jnp.maximum(jnp.sum(ex, axis=-1, keepdims=True), 1e-30)


V_NORM, V_MU0, V_W0, V_A0, V_V0, V_KK, V_KA = 0, 1, 7, 8, 9, 10, 11
MU_R, MU_W, MU_K, MU_V, MU_A, MU_G = 0, 1, 2, 3, 4, 5


def _rwkv_proj_kernel(mode, shift, tiles_per_group, has_vres, *refs):
    if mode == "r":
        (x_ref, init_ref, vec_ref, wh_ref, wl_ref, l1h, l1l, l2h, l2l,
         r_ref, lw_ref, sh_ref, carry) = refs
    elif mode == "k":
        (x_ref, init_ref, vec_ref, wh_ref, wl_ref, l1h, l1l, l2h, l2l,
         k_ref, na_ref, b_ref, carry) = refs
    elif mode == "v":
        if has_vres:
            (x_ref, init_ref, vec_ref, wh_ref, wl_ref, l1h, l1l, l2h, l2l, vf_ref,
             v_ref, carry) = refs
        else:
            (x_ref, init_ref, vec_ref, wh_ref, wl_ref, v_ref, carry) = refs
    else:
        (x_ref, init_ref, vec_ref, wh_ref, wl_ref, l1h, l1l, l2h, l2l,
         gz_ref, carry) = refs

    tt = x_ref.shape[0]
    @pl.when((pl.program_id(0) % tiles_per_group) == 0)
    def _():
        carry[...] = init_ref[...]

    h = _rms_rows(x_ref[...], vec_ref[V_NORM:V_NORM + 1, :])
    prev_rows = carry[...]
    if shift == 1:
        rolled = pltpu.roll(h, 1, axis=0)
        rowi = lax.broadcasted_iota(jnp.int32, (tt, 1), 0)
        prev = jnp.where(rowi == 0, prev_rows, rolled)
    else:
        prev = jnp.concatenate([prev_rows, h[:tt - shift, :]], axis=0)
    carry[...] = h[tt - shift:, :]
    xx = prev - h

    def mix(i):
        return h + xx * vec_ref[V_MU0 + i:V_MU0 + i + 1, :]

    def big(xm):
        return _dot3s(_split2(xm), (wh_ref[...], wl_ref[...]))

    def lora(xm, act):
        t = _dot3s(_split2(xm), (l1h[...], l1l[...]))
        return _dot3s(_split2(act(t)), (l2h[...], l2l[...]))

    if mode == "r":
        r_ref[...] = big(mix(MU_R))
        wraw = vec_ref[V_W0:V_W0 + 1, :] + lora(mix(MU_W), jnp.tanh)
        w = -_softplus(-wraw) - 0.5
        lw_ref[...] = -jnp.exp(w)
        sh_ref[...] = h[tt - shift:, :]
    elif mode == "k":
        kraw = big(mix(MU_K))
        a = _sigmoid(vec_ref[V_A0:V_A0 + 1, :] + lora(mix(MU_A), lambda t: t))
        g, e = _seg_mats(D_MODEL)
        kk = kraw * vec_ref[V_KK:V_KK + 1, :]
        ss = _segsum(kk * kk, g, e)
        kk = kk / jnp.maximum(jnp.sqrt(ss), 1e-12)
        k_ref[...] = kraw * (1.0 + (a - 1.0) * vec_ref[V_KA:V_KA + 1, :])
        na_ref[...] = -kk
        b_ref[...] = kk * a
    elif mode == "v":
        xm = mix(MU_V)
        v = big(xm)
        if has_vres:
            vg = _sigmoid(vec_ref[V_V0:V_V0 + 1, :] + lora(xm, lambda t: t))
            v = v + (vf_ref[...] - v) * vg
        v_ref[...] = v
    else:
        xm = mix(MU_G)
        z = big(xm)
        gate = lora(xm, _sigmoid)
        gz_ref[...] = gate * (z * _sigmoid(z))


def _rwkv_proj(mode, x2d, init_prev, vecs, w_pair, lora_w, vfirst, tile):
    n = x2d.shape[0]
    groups, shift, _ = init_prev.shape
    tiles_per_group = (n // groups) // tile
    has_vres = mode == "v" and lora_w is not None
    row_spec = pl.BlockSpec((tile, D_MODEL), lambda i: (i, 0))
    init_spec = pl.BlockSpec((None, shift, D_MODEL), lambda i: (i // tiles_per_group, 0, 0))
    full = lambda a: pl.BlockSpec(a.shape, lambda i: (0,) * a.ndim)
    ins = [x2d, init_prev, vecs, w_pair[0], w_pair[1]]
    in_specs = [row_spec, init_spec, full(vecs), full(w_pair[0]), full(w_pair[1])]
    if lora_w is not None:
        ins += list(lora_w)
        in_specs += [full(a) for a in lora_w]
    if has_vres:
        ins.append(vfirst)
        in_specs.append(row_spec)
    act = jax.ShapeDtypeStruct((n, D_MODEL), F32)
    if mode == "r":
        out_shape = (act, act, jax.ShapeDtypeStruct(init_prev.shape, F32))
        out_specs = (row_spec, row_spec, init_spec)
    elif mode == "k":
        out_shape = (act, act, act)
        out_specs = (row_spec, row_spec, row_spec)
    else:
        out_shape = act
        out_specs = row_spec
    return pl.pallas_call(
        functools.partial(_rwkv_proj_kernel, mode, shift, tiles_per_group, has_vres),
        grid=(n // tile,),
        in_specs=in_specs,
        out_specs=out_specs,
        out_shape=out_shape,
        scratch_shapes=[pltpu.VMEM((shift, D_MODEL), F32)],
        compiler_params=_cp("arbitrary"),
        name="rwkv_proj_" + mode,
    )(*ins)


def _wkv_kernel(chunk, n_dbl, r_ref, lw_ref, k_ref, v_ref, na_ref, b_ref, s0_ref,
                y_ref, sout_ref, s_scr):
    c = pl.program_id(2)

    @pl.when(c == 0)
    def _():
        s_scr[...] = s0_ref[...]

    lw = lw_ref[...]
    ri = lax.broadcasted_iota(jnp.int32, (chunk, chunk), 0)
    ci = lax.broadcasted_iota(jnp.int32, (chunk, chunk), 1)
    tri = jnp.where(ci <= ri, 1.0, 0.0).astype(BF16)
    cum = _xdot(tri, lw)
    total = cum[chunk - 1:chunk, :]
    w_in = jnp.exp(cum)
    w_ex = jnp.exp(cum - lw)
    w_inv = jnp.exp(-cum)
    w_rest = jnp.exp(total - cum)
    bvec = b_ref[...]
    kvec = k_ref[...]

    lane = lax.broadcasted_iota(jnp.int32, (chunk, LANE), 1)
    m0 = lane < RW_HEAD

    def stack(x):
        return jnp.concatenate([jnp.where(m0, x, 0.0), jnp.where(m0, 0.0, x)], axis=0)

    a_s = _split2(stack(na_ref[...] * w_ex))
    r_s = _split2(stack(r_ref[...] * w_in))
    b_s = _split2(stack(bvec * w_inv))
    k_s = _split2(stack(kvec * w_inv))
    v_s = _split2(stack(v_ref[...]))
    bh_s = _split2(stack(bvec * w_rest))
    kh_s = _split2(stack(kvec * w_rest))
    s_old = s_scr[...]
    s_sp = _split2(s_old)

    c2 = 2 * chunk
    r2 = lax.broadcasted_iota(jnp.int32, (c2, c2), 0)
    q2 = lax.broadcasted_iota(jnp.int32, (c2, c2), 1)
    strict = q2 < r2
    incl = q2 <= r2
    l_ab = jnp.where(strict, _dot3s(a_s, b_s, NT), 0.0)
    l_ak = jnp.where(strict, _dot3s(a_s, k_s, NT), 0.0)
    m_rb = jnp.where(incl, _dot3s(r_s, b_s, NT), 0.0)
    m_rk = jnp.where(incl, _dot3s(r_s, k_s, NT), 0.0)

    x = _dot3s(a_s, s_sp, NT) + _dot3s(_split2(l_ak), v_s)
    p = l_ab
    for it in range(n_dbl):
        p_sp = _split2(p)
        x = x + _dot3s(p_sp, _split2(x))
        if it < n_dbl - 1:
            p = _dot3s(p_sp, p_sp)
    u_s = _split2(x)
    y = _dot3s(r_s, s_sp, NT) + _dot3s(_split2(m_rb), u_s) + _dot3s(_split2(m_rk), v_s)
    y_ref[...] = y[:chunk, :] + y[chunk:, :]
    s_new = s_old * jnp.exp(total) + (_dot3s(u_s, bh_s, TN) + _dot3s(v_s, kh_s, TN))
    s_scr[...] = s_new

    @pl.when(c == pl.num_programs(2) - 1)
    def _():
        sout_ref[...] = s_new


def _wkv(r, lw, k, v, na, b, s0blk, chunk):
    g, t, _ = r.shape
    n_dbl = int(round(math.log2(chunk)))
    seq_spec = pl.BlockSpec((None, chunk, LANE), lambda gi, p, c: (gi, c, p))
    st_spec = pl.BlockSpec((None, None, LANE, LANE), lambda gi, p, c: (gi, p, 0, 0))
    return pl.pallas_call(
        functools.partial(_wkv_kernel, chunk, n_dbl),
        grid=(g, RW_PAIRS, t // chunk),
        in_specs=[seq_spec] * 6 + [st_spec],
        out_specs=(seq_spec, st_spec),
        out_shape=(jax.ShapeDtypeStruct(r.shape, F32), jax.ShapeDtypeStruct(s0blk.shape, F32)),
        scratch_shapes=[pltpu.VMEM((LANE, LANE), F32)],
        compiler_params=_cp("arbitrary", "arbitrary", "arbitrary"),
        name="wkv_scan",
    )(r, lw, k, v, na, b, s0blk)


def _rwkv_out_kernel(y_ref, r_ref, k_ref, v_ref, gz_ref, x_ref, vec_ref, wh_ref, wl_ref, o_ref):
    g, e = _seg_mats(D_MODEL)
    y = y_ref[...]
    inv = 1.0 / RW_HEAD
    mean = _segsum(y, g, e) * inv
    d = y - mean
    var = _segsum(d * d, g, e) * inv
    yn = d * lax.rsqrt(var + GN_EPS) * vec_ref[1:2, :] + vec_ref[2:3, :]
    v = v_ref[...]
    bonus = _segsum(r_ref[...] * k_ref[...] * vec_ref[0:1, :], g, e) * v
    o = (yn + bonus) * gz_ref[...]
    o_ref[...] = x_ref[...] + _dot3s(_split2(o), (wh_ref[...], wl_ref[...]))


def _rwkv_out(y, r, k, v, gz, x2d, vecs, w_pair, tile):
    n = x2d.shape[0]
    row_spec = pl.BlockSpec((tile, D_MODEL), lambda i: (i, 0))
    full = lambda a: pl.BlockSpec(a.shape, lambda i: (0,) * a.ndim)
    return pl.pallas_call(
        _rwkv_out_kernel,
        grid=(n // tile,),
        in_specs=[row_spec] * 6 + [full(vecs), full(w_pair[0]), full(w_pair[1])],
        out_specs=row_spec,
        out_shape=jax.ShapeDtypeStruct((n, D_MODEL), F32),
        compiler_params=_cp("arbitrary"),
        name="rwkv_out",
    )(y, r, k, v, gz, x2d, vecs, w_pair[0], w_pair[1])


def _wsplit(w):
    hi = w.astype(BF16)
    lo = (w - hi.astype(F32)).astype(BF16)
    return hi, lo


def _pad_to(a, axis, size):
    pad = [(0, 0)] * a.ndim
    pad[axis] = (0, size - a.shape[axis])
    return jnp.pad(a, pad)


def _lora_pair(w1, w2):
    rank = w1.shape[1]
    rp = ((rank + LANE - 1) // LANE) * LANE
    return _wsplit(_pad_to(w1, 1, rp)) + _wsplit(_pad_to(w2, 0, rp))


def _blockdiag_state(s):
    g = s.shape[0]
    s = s.reshape(g, RW_PAIRS, 2, RW_HEAD, RW_HEAD)
    z = jnp.zeros_like(s[:, :, 0])
    top = jnp.concatenate([s[:, :, 0], z], axis=-1)
    bot = jnp.concatenate([z, s[:, :, 1]], axis=-1)
    return jnp.concatenate([top, bot], axis=-2)


def _unblock_state(sb):
    g = sb.shape[0]
    a = sb[:, :, :RW_HEAD, :RW_HEAD]
    b = sb[:, :, RW_HEAD:, RW_HEAD:]
    return jnp.stack([a, b], axis=2).reshape(g, RW_HEADS, RW_HEAD, RW_HEAD)


def _rwkv_layer(x2d, init_prev, s0blk, vfirst, lw, tile, to_seq, from_seq, chunk):
    vecs, w_r, w_k, w_v, w_z, lo_w, lo_a, lo_v, lo_g, vecs_out, w_o = lw
    r, logw, shift_out = _rwkv_proj("r", x2d, init_prev, vecs, w_r, lo_w, None, tile)
    k, na, b = _rwkv_proj("k", x2d, init_prev, vecs, w_k, lo_a, None, tile)
    v = _rwkv_proj("v", x2d, init_prev, vecs, w_v, lo_v, vfirst, tile)
    gz = _rwkv_proj("z", x2d, init_prev, vecs, w_z, lo_g, None, tile)
    if vfirst is None:
        vfirst = v
    y, sblk = _wkv(*(to_seq(t) for t in (r, logw, k, v, na, b)), s0blk, chunk)
    x_new = _rwkv_out(from_seq(y), r, k, v, gz, x2d, vecs_out, w_o, tile)
    return x_new, shift_out, sblk, vfirst


def _rwkv_weights(j, norm_w_i, rw_mu, rw_w_rkvz, rw_w0, rw_w1, rw_w2, rw_a0, rw_a1, rw_a2, rw_v0, rw_v1,
                  rw_v2, rw_g1, rw_g2, rw_k_k, rw_k_a, rw_r_k, rw_ln_w, rw_ln_b, rw_w_o):
    zero = jnp.zeros((D_MODEL,), F32)
    v0 = rw_v0[j - 1] if j > 0 else zero
    vecs = jnp.stack([norm_w_i] + [rw_mu[j, i] for i in range(6)]
                     + [rw_w0[j], rw_a0[j], v0, rw_k_k[j], rw_k_a[j]] + [zero] * 4)
    lo_v = _lora_pair(rw_v1[j - 1], rw_v2[j - 1]) if j > 0 else None
    vecs_out = jnp.stack([rw_r_k[j].reshape(D_MODEL), rw_ln_w[j], rw_ln_b[j]] + [zero] * 5)
    return (vecs, _wsplit(rw_w_rkvz[j, 0]), _wsplit(rw_w_rkvz[j, 1]), _wsplit(rw_w_rkvz[j, 2]),
            _wsplit(rw_w_rkvz[j, 3]), _lora_pair(rw_w1[j], rw_w2[j]), _lora_pair(rw_a1[j], rw_a2[j]),
            lo_v, _lora_pair(rw_g1[j], rw_g2[j]), vecs_out, _wsplit(rw_w_o[j]))


def _seg_rms(y, aux_row, width):
    g, e = _seg_mats(width)
    ms = _segsum(y * y, g, e) * (1.0 / NSA_HD)
    return y * lax.rsqrt(ms + RMS_EPS) * aux_row


def _nsa_proj_kernel(mode, x_ref, nw_ref, wh_ref, wl_ref, aux_ref, *out_refs):
    h = _rms_rows(x_ref[...], nw_ref[...])
    y = _dot3s(_split2(h), (wh_ref[...], wl_ref[...]))
    if mode == "q":
        out_refs[0][...] = _seg_rms(y, aux_ref[0:1, :], D_MODEL) * (NSA_HD ** -0.5)
    elif mode == "kv":
        kc, vc, ks, vs, kw, vw = (y[:, i * KV_DIM:(i + 1) * KV_DIM] for i in range(6))
        out_refs[0][...] = kc
        out_refs[1][...] = vc
        out_refs[2][...] = _seg_rms(ks, aux_ref[0:1, :], KV_DIM)
        out_refs[3][...] = vs
        out_refs[4][...] = _seg_rms(kw, aux_ref[1:2, :], KV_DIM)
        out_refs[5][...] = vw
    elif mode == "z":
        out_refs[0][...] = y * _sigmoid(y)
    else:
        out_refs[0][...] = _sigmoid(y)


def _nsa_proj(mode, x2d, nw_row, w_pair, aux, tile):
    n = x2d.shape[0]
    width = w_pair[0].shape[1]
    row_spec = pl.BlockSpec((tile, D_MODEL), lambda i: (i, 0))
    full = lambda a: pl.BlockSpec(a.shape, lambda i: (0,) * a.ndim)
    if mode == "kv":
        out_shape = tuple(jax.ShapeDtypeStruct((n, KV_DIM), F32) for _ in range(6))
        out_specs = tuple(pl.BlockSpec((tile, KV_DIM), lambda i: (i, 0)) for _ in range(6))
    else:
        out_shape = (jax.ShapeDtypeStruct((n, width), F32),)
        out_specs = (pl.BlockSpec((tile, width), lambda i: (i, 0)),)
    res = pl.pallas_call(
        functools.partial(_nsa_proj_kernel, mode),
        grid=(n // tile,),
        in_specs=[row_spec, full(nw_row), full(w_pair[0]), full(w_pair[1]), full(aux)],
        out_specs=out_specs,
        out_shape=out_shape,
        compiler_params=_cp("arbitrary"),
        name="nsa_proj_" + mode,
    )(x2d, nw_row, w_pair[0], w_pair[1], aux)
    return res if mode == "kv" else res[0]


def _compress_kernel(n_prefetch, n_src, m, do_norm, *refs):
    refs = refs[n_prefetch:]
    src_lo = refs[:n_src]
    src_hi = refs[n_src:2 * n_src]
    pos_ref, wah, wal, wbh, wbl, w2h, w2l, nrm_ref, o_ref = refs[2 * n_src:]
    rows_per_src = m // n_src
    acc_a = jnp.zeros((m, KV_DIM), F32)
    acc_b = jnp.zeros((m, KV_DIM), F32)

    def strided(srcs, jj):
        parts = [s[pl.ds(jj, rows_per_src, stride=CMP_STRIDE), :] for s in srcs]
        return parts[0] if n_src == 1 else jnp.concatenate(parts, axis=0)

    for jj in range(CMP_STRIDE):
        x = jnp.concatenate([strided(src_lo, jj), strided(src_hi, jj)], axis=-1)
        xa = _split2(x + pos_ref[jj:jj + 1, :])
        xb = _split2(x + pos_ref[CMP_STRIDE + jj:CMP_STRIDE + jj + 1, :])
        acc_a = acc_a + _dot3s(xa, (wah[jj], wal[jj]))
        acc_b = acc_b + _dot3s(xb, (wbh[jj], wbl[jj]))
    pre = acc_a + pltpu.roll(acc_b, m - 1, axis=0)
    hid = pre * _sigmoid(pre)
    out = _dot3s(_split2(hid), (w2h[...], w2l[...]))
    if do_norm:
        out = _seg_rms(out, nrm_ref[0:1, :], KV_DIM)
    o_ref[...] = out


def _compress_call(srcs, half_specs, grid, n_prefetch, prefetch, cw, do_norm, m, batch, idx):
    pos, wa, wb, w2, nrm = cw
    consts = [pos, wa[0], wa[1], wb[0], wb[1], w2[0], w2[1], nrm]
    cspecs = [pl.BlockSpec(a.shape, functools.partial(lambda nd, *_: (0,) * nd, a.ndim)) for a in consts]
    out_spec = pl.BlockSpec((None, m, KV_DIM), idx)
    gs = pltpu.PrefetchScalarGridSpec(
        num_scalar_prefetch=n_prefetch, grid=grid, in_specs=half_specs(0) + half_specs(1) + cspecs,
        out_specs=out_spec)
    return pl.pallas_call(
        functools.partial(_compress_kernel, n_prefetch, len(srcs), m, do_norm),
        grid_spec=gs,
        out_shape=jax.ShapeDtypeStruct((batch, m, KV_DIM), F32),
        compiler_params=_cp("arbitrary"),
        name="nsa_compress",
    )(*prefetch, *srcs, *srcs, *consts)


def _compress_weights(pos, w1, w2, knorm_row):
    eye = jnp.eye(NSA_KV, dtype=F32)
    w1r = w1.reshape(CMP_BLOCK, NSA_HD, NSA_HD)
    blk = jnp.einsum("ab,jde->jadbe", eye, w1r).reshape(CMP_BLOCK, KV_DIM, KV_DIM)
    w2b = jnp.einsum("ab,de->adbe", eye, w2).reshape(KV_DIM, KV_DIM)
    nrm = jnp.zeros((8, KV_DIM), F32).at[0].set(jnp.tile(knorm_row, NSA_KV))
    return (jnp.tile(pos, (1, NSA_KV)), _wsplit(blk[:CMP_STRIDE]), _wsplit(blk[CMP_STRIDE:]), _wsplit(w2b), nrm)


def _bucket_thresholds():
    n = np.arange(0, 8192, dtype=np.float64)
    max_exact = NUM_BUCKETS // 2
    large = max_exact + np.floor(
        np.log(np.maximum(n, 1.0) / max_exact) / math.log(MAX_DISTANCE / max_exact) * (NUM_BUCKETS - max_exact))
    bucket = np.where(n < max_exact, n, np.minimum(large, NUM_BUCKETS - 1)).astype(np.int64)
    return [int(np.argmax(bucket >= m)) for m in range(NUM_BUCKETS)]


def _bias_kernel(thr, base_mul, base_add, col_stride, tab_ref, o_ref):
    i = pl.program_id(0)
    h = pl.program_id(1)
    rows, cols = o_ref.shape
    r = lax.broadcasted_iota(jnp.int32, (rows, cols), 0)
    c = lax.broadcasted_iota(jnp.int32, (rows, cols), 1)
    d = (i * base_mul + base_add) + r - c * col_stride
    val = jnp.full((rows, cols), tab_ref[0, h], F32)
    for m in range(1, NUM_BUCKETS):
        val = jnp.where(d >= thr[m], tab_ref[m, h], val)
    o_ref[...] = val


def _bias_table(rel_bias, n_i, rows, cols, base_mul, base_add, col_stride):
    return pl.pallas_call(
        functools.partial(_bias_kernel, _bucket_thresholds(), base_mul, base_add, col_stride),
        grid=(n_i, NSA_HEADS),
        in_specs=[pl.BlockSpec(memory_space=pltpu.SMEM)],
        out_specs=pl.BlockSpec((None, None, rows, cols), lambda i, h: (i, h, 0, 0)),
        out_shape=jax.ShapeDtypeStruct((n_i, NSA_HEADS, rows, cols), F32),
        compiler_params=_cp("arbitrary", "arbitrary"),
        name="nsa_bias_table",
    )(rel_bias)


def _overlap_t(n_cmp, n_cmp_pad, n_sel, n_sel_pad):
    cs = np.arange(n_cmp_pad)[None, :] * CMP_STRIDE
    ss = np.arange(n_sel_pad)[:, None] * SEL_BLOCK
    ov = np.maximum(np.minimum(cs + CMP_BLOCK, ss + SEL_BLOCK) - np.maximum(cs, ss), 0) / CMP_BLOCK
    ov = ov * (np.arange(n_cmp_pad)[None, :] < n_cmp) * (np.arange(n_sel_pad)[:, None] < n_sel)
    return jnp.asarray(ov, F32)


def _online_update(carry, q_sp, k_tile, v_tile, add_tile):
    m, l, acc = carry
    lf = _dot3s(q_sp, _split2(k_tile), NT) + add_tile
    m_new = jnp.maximum(m, jnp.max(lf, axis=-1, keepdims=True))
    m_safe = jnp.where(m_new == NEG_INF, 0.0, m_new)
    alpha = jnp.exp(m - m_safe)
    p = jnp.exp(lf - m_safe)
    l = alpha * l + jnp.sum(p, axis=-1, keepdims=True)
    acc = alpha * acc + _dot3(p, v_tile)
    return m_new, l, acc


def _attn_init(rows):
    return (jnp.full((rows, 1), NEG_INF, F32), jnp.zeros((rows, 1), F32), jnp.zeros((rows, LANE), F32))


def _attn_finish(carry):
    _, l, acc = carry
    return acc / jnp.maximum(l, 1e-30)


def _select_blocks(imp_t, qpos_lane, sc_scr):
    n_sel = imp_t.shape[0]
    sidx = lax.broadcasted_iota(jnp.int32, imp_t.shape, 0)
    cur = lax.shift_right_logical(qpos_lane, 6)
    future = sidx * SEL_BLOCK > qpos_lane
    forced = (sidx == 0) | (sidx == cur) | (sidx == cur - 1)
    score = jnp.where(future, NEG_INF, jnp.where(forced, float("inf"), imp_t))
    sc_scr[...] = score

    def body(s2, cnt):
        rowv = sc_scr[pl.ds(s2, 1), :]
        ahead = (rowv > score) | ((rowv == score) & (s2 < sidx))
        return cnt + jnp.where(ahead, 1.0, 0.0)

    cnt = lax.fori_loop(0, n_sel, body, jnp.zeros(imp_t.shape, F32))
    return jnp.where(cnt < SEL_TOPN, 1.0, 0.0).astype(BF16)


def _block_mask(sel_t, kt):
    n_sel = sel_t.shape[0]
    sr = lax.broadcasted_iota(jnp.int32, (n_sel, LANE), 0)
    kc = lax.broadcasted_iota(jnp.int32, (n_sel, LANE), 1)
    expand = jnp.where(sr == 2 * kt + lax.shift_right_logical(kc, 6), 1.0, 0.0).astype(BF16)
    return _dg(sel_t, expand, TN)


def _lane_mask(par, rows):
    lane = lax.broadcasted_iota(jnp.int32, (rows, LANE), 1)
    lo = par * NSA_HD
    return (lane >= lo) & (lane < lo + NSA_HD)


def _nsa_prompt_kernel(n_sel, mc, *refs):
    q_refs = refs[0:4]
    gate_ref, kc_ref, vc_ref, ks_ref, vs_ref = refs[4:9]
    kw_refs = refs[9:14]
    vw_refs = refs[14:19]
    toep_ref, cb_ref, ovt_ref = refs[19:22]
    o_refs = refs[22:26]
    sc_scr = refs[26]

    qb = pl.program_id(1)
    par = pl.program_id(2) % 2
    mk = _lane_mask(par, LANE)
    q = jnp.concatenate([jnp.where(mk, r[...], 0.0) for r in q_refs], axis=0)
    q_sp = _split2(q)
    rows = NSA_GROUP * LANE
    qpos = qb * LANE + (lax.broadcasted_iota(jnp.int32, (rows, 1), 0) & (LANE - 1))

    cend = lax.broadcasted_iota(jnp.int32, (1, mc), 1) * CMP_STRIDE + (CMP_BLOCK - 1)
    lc = _dot3s(q_sp, _split2(kc_ref[...]), NT) + jnp.concatenate([cb_ref[g] for g in range(NSA_GROUP)], axis=0)
    p_c = _masked_softmax_rows(lc, cend <= qpos)
    o_c = _dot3(p_c, vc_ref[...])
    psum = (p_c[0:LANE] + p_c[LANE:2 * LANE]) + (p_c[2 * LANE:3 * LANE] + p_c[3 * LANE:4 * LANE])
    imp_t = _dot3(ovt_ref[...], psum, NT)
    qpos_lane = qb * LANE + lax.broadcasted_iota(jnp.int32, (n_sel, LANE), 1)
    sel_t = _select_blocks(imp_t, qpos_lane, sc_scr)

    qrow = qb * LANE + lax.broadcasted_iota(jnp.int32, (LANE, LANE), 0)
    kcol = lax.broadcasted_iota(jnp.int32, (LANE, LANE), 1)

    def sel_body(kt, carry):
        start = pl.multiple_of(kt * LANE, LANE)
        ok = (_block_mask(sel_t, kt) > 0.5) & (kt * LANE + kcol <= qrow)
        madd = jnp.where(ok, 0.0, NEG_INF)
        ti = jnp.minimum(qb - kt, 8)
        add = jnp.concatenate([toep_ref[ti, g] + madd for g in range(NSA_GROUP)], axis=0)
        return _online_update(carry, q_sp, ks_ref[pl.ds(start, LANE), :], vs_ref[pl.ds(start, LANE), :], add)

    o_s = _attn_finish(lax.fori_loop(0, qb + 1, sel_body, _attn_init(rows)))

    carry = _attn_init(rows)
    for i in range(5):
        kpos = (qb - 4 + i) * LANE + kcol
        dist = qrow - kpos
        ok = (kpos >= 0) & (dist >= 0) & (dist < WINDOW)
        madd = jnp.where(ok, 0.0, NEG_INF)
        add = jnp.concatenate([toep_ref[4 - i, g] + madd for g in range(NSA_GROUP)], axis=0)
        carry = _online_update(carry, q_sp, kw_refs[i][...], vw_refs[i][...], add)
    o_w = _attn_finish(carry)

    for g in range(NSA_GROUP):
        sl = slice(g * LANE, (g + 1) * LANE)
        og = (gate_ref[:, 3 * g:3 * g + 1] * o_c[sl] + gate_ref[:, 3 * g + 1:3 * g + 2] * o_s[sl]
              + gate_ref[:, 3 * g + 2:3 * g + 3] * o_w[sl])
        og = jnp.where(mk, og, 0.0)
        o_ref = o_refs[g]

        @pl.when(par == 0)
        def _():
            o_ref[...] = og

        @pl.when(par == 1)
        def _():
            o_ref[...] = o_ref[...] + og


def _nsa_prompt_attn(qp, gates, kcmp, vcmp, ks, vs, kw, vw, toep, cbias, batch, seq):
    nqb = seq // LANE
    mc = seq // CMP_STRIDE
    n_sel = seq // SEL_BLOCK
    ovt = _overlap_t(mc - 1, mc, n_sel, n_sel)
    q_specs = [pl.BlockSpec((LANE, LANE), functools.partial(lambda g, b, qb, kh: (b * nqb + qb, 2 * g + kh // 2), g))
               for g in range(NSA_GROUP)]
    half = lambda rows: pl.BlockSpec((None, rows, LANE), lambda b, qb, kh: (b, 0, kh // 2))
    win_specs = [pl.BlockSpec((None, LANE, LANE),
                              functools.partial(lambda i, b, qb, kh: (b, jnp.maximum(qb - 4 + i, 0), kh // 2), i))
                 for i in range(5)]
    in_specs = (q_specs
                + [pl.BlockSpec((LANE, LANE), lambda b, qb, kh: (b * nqb + qb, kh))]
                + [half(mc), half(mc), half(seq), half(seq)]
                + win_specs + win_specs
                + [pl.BlockSpec((9, NSA_GROUP, LANE, LANE), lambda b, qb, kh: (0, kh, 0, 0)),
                   pl.BlockSpec((None, NSA_GROUP, LANE, mc), lambda b, qb, kh: (qb, kh, 0, 0)),
                   pl.BlockSpec(ovt.shape, lambda b, qb, kh: (0, 0))])
    out_spec = pl.BlockSpec((LANE, LANE), lambda b, qb, kh: (b * nqb + qb, kh // 2))
    n = batch * seq
    return pl.pallas_call(
        functools.partial(_nsa_prompt_kernel, n_sel, mc),
        grid=(batch, nqb, NSA_KV),
        in_specs=in_specs,
        out_specs=(out_spec,) * NSA_GROUP,
        out_shape=tuple(jax.ShapeDtypeStruct((n, KV_DIM), F32) for _ in range(NSA_GROUP)),
        scratch_shapes=[pltpu.VMEM((n_sel, LANE), F32)],
        compiler_params=_cp("arbitrary", "arbitrary", "arbitrary"),
        name="nsa_prompt_attn",
    )(qp, qp, qp, qp, gates, kcmp, vcmp, ks, vs, *([kw] * 5), *([vw] * 5), toep, cbias, ovt)


T_PAD = 8


def _nsa_sample_kernel(past, dec_seq, n_pages, win_buf, *refs):
    n = n_pages
    t_new = dec_seq
    q_refs = refs[1:5]
    gate_ref, kc_ref, vc_ref = refs[5:8]
    ks_pages = refs[8:8 + n]
    vs_pages = refs[8 + n:8 + 2 * n]
    ksn, vsn, kwn, vwn, wk_ref, wv_ref, sb_ref, cb_ref, ovt_ref = refs[8 + 2 * n:17 + 2 * n]
    o_refs = refs[17 + 2 * n:21 + 2 * n]
    wko_ref, wvo_ref, sc_scr = refs[21 + 2 * n:24 + 2 * n]

    par = pl.program_id(1) % 2
    rows = NSA_GROUP * T_PAD
    mk8 = _lane_mask(par, T_PAD)
    q = jnp.concatenate([jnp.where(mk8, r[...], 0.0) for r in q_refs], axis=0)
    q_sp = _split2(q)
    qpos = past + (lax.broadcasted_iota(jnp.int32, (rows, 1), 0) & (T_PAD - 1))
    qpos8 = past + lax.broadcasted_iota(jnp.int32, (T_PAD, LANE), 0)
    kcol = lax.broadcasted_iota(jnp.int32, (T_PAD, LANE), 1)
    zpad = jnp.zeros((LANE - T_PAD, LANE), F32)

    def per_group(fn):
        return jnp.concatenate([fn(g) for g in range(NSA_GROUP)], axis=0)

    mc = kc_ref.shape[0]
    cend = lax.broadcasted_iota(jnp.int32, (1, mc), 1) * CMP_STRIDE + (CMP_BLOCK - 1)
    lc = _dot3s(q_sp, _split2(kc_ref[...]), NT) + per_group(lambda g: cb_ref[g])
    p_c = _masked_softmax_rows(lc, cend <= qpos)
    o_c = _dot3(p_c, vc_ref[...])
    psum = (p_c[0:T_PAD] + p_c[T_PAD:2 * T_PAD]) + (p_c[2 * T_PAD:3 * T_PAD] + p_c[3 * T_PAD:4 * T_PAD])
    imp_t = _dot3(ovt_ref[...], jnp.concatenate([psum, zpad], axis=0), NT)
    qpos_lane = past + lax.broadcasted_iota(jnp.int32, imp_t.shape, 1)
    sel_t = _select_blocks(imp_t, qpos_lane, sc_scr)

    def tile_add(kt, base, extra_ok):
        kpos = base + kcol
        ok = (kpos <= qpos8) & extra_ok(kpos)
        if kt is not None:
            ok = ok & (_block_mask(sel_t, kt)[:T_PAD, :] > 0.5)
        madd = jnp.where(ok, 0.0, NEG_INF)
        return per_group(lambda g: sb_ref[g, :, base:base + LANE] + madd)

    always = lambda kpos: kpos >= 0
    carry = _attn_init(rows)
    for p in range(n):
        carry = _online_update(carry, q_sp, ks_pages[p][...], vs_pages[p][...], tile_add(p, p * PAGE, always))
    k_new = jnp.concatenate([ksn[...], zpad], axis=0)
    v_new = jnp.concatenate([vsn[...], zpad], axis=0)
    carry = _online_update(carry, q_sp, k_new, v_new, tile_add(n, past, always))
    o_s = _attn_finish(carry)

    in_win = lambda kpos: (qpos8 - kpos) < WINDOW
    carry = _attn_init(rows)
    w0 = past - win_buf
    for i in range(win_buf // LANE):
        carry = _online_update(carry, q_sp, wk_ref[i * LANE:(i + 1) * LANE, :], wv_ref[i * LANE:(i + 1) * LANE, :],
                               tile_add(None, w0 + i * LANE, in_win))
    kw_new = jnp.concatenate([kwn[...], zpad], axis=0)
    vw_new = jnp.concatenate([vwn[...], zpad], axis=0)
    carry = _online_update(carry, q_sp, kw_new, vw_new, tile_add(None, past, in_win))
    o_w = _attn_finish(carry)

    for g in range(NSA_GROUP):
        sl = slice(g * T_PAD, (g + 1) * T_PAD)
        og = (gate_ref[:, 3 * g:3 * g + 1] * o_c[sl] + gate_ref[:, 3 * g + 1:3 * g + 2] * o_s[sl]
              + gate_ref[:, 3 * g + 2:3 * g + 3] * o_w[sl])
        og = jnp.where(mk8, og, 0.0)
        o_ref = o_refs[g]

        @pl.when(par == 0)
        def _():
            o_ref[...] = og

        @pl.when(par == 1)
        def _():
            o_ref[...] = o_ref[...] + og

    rowi = lax.broadcasted_iota(jnp.int32, (LANE, LANE), 0)
    for src, new, dst in ((wk_ref, kw_new, wko_ref), (wv_ref, vw_new, wvo_ref)):
        shifted = pltpu.roll(src[...], win_buf - t_new, axis=0)
        tail = pltpu.roll(new, LANE - t_new, axis=0)
        dst[0:win_buf - LANE, :] = shifted[0:win_buf - LANE, :]
        dst[win_buf - LANE:win_buf, :] = jnp.where(rowi >= LANE - t_new, tail, shifted[win_buf - LANE:win_buf, :])


def _nsa_sample_attn(j, page_table, qp8, gates8, kcmp, vcmp, cache_k, cache_v, new8, win_k, win_v, sbias, scbias,
                     past, dec_seq):
    batch, n_pages = page_table.shape
    win_buf = win_k.shape[2]
    mc = kcmp.shape[1]
    n_sel = -(-(past + dec_seq) // SEL_BLOCK)
    n_sel_pad = -(-n_sel // 8) * 8
    ovt = _overlap_t(mc - 1, mc, n_sel, n_sel_pad)
    q_specs = [pl.BlockSpec((None, T_PAD, LANE), functools.partial(lambda g, b, kh, pt: (b, 0, 2 * g + kh // 2), g))
               for g in range(NSA_GROUP)]
    half = lambda rows: pl.BlockSpec((None, rows, LANE), lambda b, kh, pt: (b, 0, kh // 2))
    page_specs = [pl.BlockSpec((None, None, PAGE, LANE),
                               functools.partial(lambda p, b, kh, pt: (j, pt[b, p], 0, kh // 2), p))
                  for p in range(n_pages)]
    win_spec = pl.BlockSpec((None, None, win_buf, LANE), lambda b, kh, pt: (j, b, 0, kh // 2))
    in_specs = (q_specs
                + [pl.BlockSpec((None, T_PAD, LANE), lambda b, kh, pt: (b, 0, kh)), half(mc), half(mc)]
                + page_specs + page_specs
                + [half(T_PAD)] * 4 + [win_spec, win_spec]
                + [pl.BlockSpec((NSA_GROUP, T_PAD, sbias.shape[2]), lambda b, kh, pt: (kh, 0, 0)),
                   pl.BlockSpec((NSA_GROUP, T_PAD, mc), lambda b, kh, pt: (kh, 0, 0)),
                   pl.BlockSpec(ovt.shape, lambda b, kh, pt: (0, 0))])
    out_specs = (half(T_PAD),) * NSA_GROUP + (half(win_buf), half(win_buf))
    out_shape = (tuple(jax.ShapeDtypeStruct((batch, T_PAD, KV_DIM), F32) for _ in range(NSA_GROUP))
                 + tuple(jax.ShapeDtypeStruct((batch, win_buf, KV_DIM), F32) for _ in range(2)))
    gs = pltpu.PrefetchScalarGridSpec(
        num_scalar_prefetch=1, grid=(batch, NSA_KV), in_specs=in_specs, out_specs=out_specs,
        scratch_shapes=[pltpu.VMEM((n_sel_pad, LANE), F32)])
    res = pl.pallas_call(
        functools.partial(_nsa_sample_kernel, past, dec_seq, n_pages, win_buf),
        grid_spec=gs,
        out_shape=out_shape,
        compiler_params=_cp("arbitrary", "arbitrary"),
        name="nsa_sample_attn",
    )(page_table, qp8, qp8, qp8, qp8, gates8, kcmp, vcmp, *([cache_k] * n_pages), *([cache_v] * n_pages),
      *new8, win_k, win_v, sbias, scbias, ovt)
    return res[:NSA_GROUP], res[NSA_GROUP], res[NSA_GROUP + 1]


def _nsa_sample_layer(j, x2d, nw, caches, wins, page_table, sbias, scbias, dec_seq, past):
    batch, n_pages = page_table.shape
    n = x2d.shape[0]
    qp, kvs, sz, gates = _nsa_project_all(x2d, nw, n)
    cw_k, cw_v, w_o = nw[7:]
    m = n_pages * PAGE // CMP_STRIDE
    page_specs = lambda h: [pl.BlockSpec((None, None, PAGE, LANE),
                                         functools.partial(lambda p, b, pt: (j, pt[b, p], 0, h), p))
                            for p in range(n_pages)]
    out_idx = lambda b, pt: (b, 0, 0)
    kcmp = _compress_call([caches[0]] * n_pages, page_specs, (batch,), 1, (page_table,), cw_k, True, m, batch, out_idx)
    vcmp = _compress_call([caches[1]] * n_pages, page_specs, (batch,), 1, (page_table,), cw_v, False, m, batch, out_idx)
    pad8 = lambda a: _pad_to(a.reshape(batch, dec_seq, a.shape[-1]), 1, T_PAD)
    new8 = tuple(pad8(kvs[i]) for i in (2, 3, 4, 5))
    o8, win_k_new, win_v_new = _nsa_sample_attn(j, page_table, pad8(qp), pad8(gates), kcmp, vcmp, caches[2], caches[3],
                                                new8, wins[0], wins[1], sbias, scbias, past, dec_seq)
    o_parts = [o[:, :dec_seq].reshape(n, KV_DIM) for o in o8]
    x_new = _nsa_out(o_parts, sz, x2d, w_o, n)
    return x_new, kvs[:4] + (win_k_new, win_v_new)


def _nsa_out_kernel(o0, o1, o2, o3, sz_ref, x_ref, wh_ref, wl_ref, out_ref):
    o = jnp.concatenate([o0[...], o1[...], o2[...], o3[...]], axis=-1) * sz_ref[...]
    out_ref[...] = x_ref[...] + _dot3s(_split2(o), (wh_ref[...], wl_ref[...]))


def _nsa_out(o_parts, sz, x2d, w_pair, tile):
    n = x2d.shape[0]
    row_spec = pl.BlockSpec((tile, D_MODEL), lambda i: (i, 0))
    part_spec = pl.BlockSpec((tile, KV_DIM), lambda i: (i, 0))
    full = lambda a: pl.BlockSpec(a.shape, lambda i: (0,) * a.ndim)
    return pl.pallas_call(
        _nsa_out_kernel,
        grid=(n // tile,),
        in_specs=[part_spec] * 4 + [row_spec, row_spec, full(w_pair[0]), full(w_pair[1])],
        out_specs=row_spec,
        out_shape=jax.ShapeDtypeStruct((n, D_MODEL), F32),
        compiler_params=_cp("arbitrary"),
        name="nsa_out",
    )(*o_parts, sz, x2d, w_pair[0], w_pair[1])


def _nsa_weights(j, norm_w_i, nsa_w_in, nsa_q_norm, nsa_k_norm, nsa_cmp_pos, nsa_cmp_w1, nsa_cmp_w2, nsa_w_o):
    w_in = nsa_w_in[j]
    qd = NSA_HEADS * NSA_HD

    def perm_cols(w):
        return w.reshape(D_MODEL, NSA_KV, NSA_GROUP, NSA_HD).transpose(0, 2, 1, 3).reshape(D_MODEL, qd)

    w_q = perm_cols(w_in[:, :qd])
    w_kv = w_in[:, qd:qd + 6 * KV_DIM]
    w_z = perm_cols(w_in[:, qd + 6 * KV_DIM:2 * qd + 6 * KV_DIM])
    w_g = w_in[:, 2 * qd + 6 * KV_DIM:].reshape(D_MODEL, NSA_KV, 3 * NSA_GROUP)
    w_g = _pad_to(w_g, 2, LANE).reshape(D_MODEL, NSA_KV * LANE)
    w_o = nsa_w_o[j].reshape(NSA_KV, NSA_GROUP, NSA_HD, D_MODEL).transpose(1, 0, 2, 3).reshape(qd, D_MODEL)
    aux_q = jnp.zeros((8, qd), F32).at[0].set(jnp.tile(nsa_q_norm[j], NSA_HEADS))
    aux_kv = (jnp.zeros((8, KV_DIM), F32).at[0].set(jnp.tile(nsa_k_norm[j, 0], NSA_KV))
              .at[1].set(jnp.tile(nsa_k_norm[j, 1], NSA_KV)))
    cw_k = _compress_weights(nsa_cmp_pos[j, 0], nsa_cmp_w1[j, 0], nsa_cmp_w2[j, 0], nsa_k_norm[j, 2])
    cw_v = _compress_weights(nsa_cmp_pos[j, 1], nsa_cmp_w1[j, 1], nsa_cmp_w2[j, 1], nsa_k_norm[j, 2])
    return (norm_w_i[None, :], _wsplit(w_q), _wsplit(w_kv), _wsplit(w_z), _wsplit(w_g), aux_q, aux_kv,
            cw_k, cw_v, _wsplit(w_o))


def _nsa_project_all(x2d, nw, tile):
    nw_row, w_q, w_kv, w_z, w_g, aux_q, aux_kv = nw[:7]
    qp = _nsa_proj("q", x2d, nw_row, w_q, aux_q, tile)
    kvs = _nsa_proj("kv", x2d, nw_row, w_kv, aux_kv, tile)
    sz = _nsa_proj("z", x2d, nw_row, w_z, aux_q, tile)
    gates = _nsa_proj("g", x2d, nw_row, w_g, aux_q, tile)
    return qp, kvs, sz, gates


def _nsa_prompt_layer(x2d, nw, toep, cbias, batch, seq, tile):
    qp, (kc, vc, ks, vs, kw, vw), sz, gates = _nsa_project_all(x2d, nw, tile)
    cw_k, cw_v, w_o = nw[7:]
    mc = seq // CMP_STRIDE
    b3 = lambda a: a.reshape(batch, seq, KV_DIM)
    src_spec = lambda h: [pl.BlockSpec((None, seq, LANE), lambda b: (b, 0, h))]
    kcmp = _compress_call([b3(kc)], src_spec, (batch,), 0, (), cw_k, True, mc, batch, lambda b: (b, 0, 0))
    vcmp = _compress_call([b3(vc)], src_spec, (batch,), 0, (), cw_v, False, mc, batch, lambda b: (b, 0, 0))
    o_parts = _nsa_prompt_attn(qp, gates, kcmp, vcmp, b3(ks), b3(vs), b3(kw), b3(vw), toep, cbias, batch, seq)
    x_new = _nsa_out(o_parts, sz, x2d, w_o, tile)
    return x_new, (kc, vc, ks, vs, kw, vw)


PROMPT_TILE = 512
PROMPT_CHUNK = 64
SAMPLE_CHUNK = 8


def kernel(x_prompt, x_sample, cache_cmp_k, cache_cmp_v, cache_sel_k, cache_sel_v, state_win_k, state_win_v, state_wkv, state_shift, page_table, norm_w, rel_bias, rw_mu, rw_w_rkvz, rw_w0, rw_w1, rw_w2, rw_a0, rw_a1, rw_a2, rw_v0, rw_v1, rw_v2, rw_g1, rw_g2, rw_k_k, rw_k_a, rw_r_k, rw_ln_w, rw_ln_b, rw_w_o, nsa_w_in, nsa_q_norm, nsa_k_norm, nsa_cmp_pos, nsa_cmp_w1, nsa_cmp_w2, nsa_w_o):
    bp, tp, d = x_prompt.shape
    bs, ts, _ = x_sample.shape
    depth = norm_w.shape[0]
    n_pages = page_table.shape[1]
    past = n_pages * PAGE
    win_buf = state_win_k.shape[2]
    tile = min(PROMPT_TILE, tp)
    ns = bs * ts

    xp = x_prompt.reshape(bp * tp, d)
    xs = x_sample.reshape(ns, d)
    flat = lambda c: c.reshape(c.shape[0], c.shape[1], c.shape[2], KV_DIM)
    caches = tuple(flat(c) for c in (cache_cmp_k, cache_cmp_v, cache_sel_k, cache_sel_v))
    wins = (flat(state_win_k), flat(state_win_v))

    toep = _bias_table(rel_bias, 9, LANE, LANE, LANE, 0, 1)
    cbias = _bias_table(rel_bias, tp // LANE, LANE, tp // CMP_STRIDE, LANE, -(CMP_BLOCK - 1), CMP_STRIDE)
    sbias = _bias_table(rel_bias, 1, T_PAD, past + LANE, 0, past, 1)[0]
    scbias = _bias_table(rel_bias, 1, T_PAD, past // CMP_STRIDE, 0, past - (CMP_BLOCK - 1), CMP_STRIDE)[0]

    to_tm = lambda a: a.reshape(bs, ts, d).transpose(1, 0, 2).reshape(ns, d)
    from_tm = lambda a: a.reshape(ts, bs, d).transpose(1, 0, 2).reshape(ns, d)
    p_to_seq = lambda a: a.reshape(bp, tp, d)
    p_from_seq = lambda a: a.reshape(bp * tp, d)
    s_to_seq = lambda a: _pad_to(a.reshape(ts, bs, d).transpose(1, 0, 2), 1, SAMPLE_CHUNK)
    s_from_seq = lambda a: a[:, :ts].transpose(1, 0, 2).reshape(ns, d)

    vf_p = vf_s = None
    p_wkv, p_shift, s_wkv, s_shift = [], [], [], []
    p_kv = [[] for _ in range(6)]
    s_kv = [[] for _ in range(6)]
    for i in range(depth):
        j = i // 2
        if i % 2 == 0:
            lw = _rwkv_weights(j, norm_w[i], rw_mu, rw_w_rkvz, rw_w0, rw_w1, rw_w2, rw_a0, rw_a1, rw_a2, rw_v0,
                               rw_v1, rw_v2, rw_g1, rw_g2, rw_k_k, rw_k_a, rw_r_k, rw_ln_w, rw_ln_b, rw_w_o)
            xp, sh, sblk, vf_p = _rwkv_layer(xp, jnp.zeros((bp, 1, d), F32), jnp.zeros((bp, RW_PAIRS, LANE, LANE), F32),
                                             vf_p, lw, tile, p_to_seq, p_from_seq, PROMPT_CHUNK)
            p_wkv.append(_unblock_state(sblk))
            p_shift.append(sh[:, 0])
            xs_tm, sh, sblk, vf_s = _rwkv_layer(to_tm(xs), state_shift[j][None], _blockdiag_state(state_wkv[j]),
                                                vf_s, lw, ns, s_to_seq, s_from_seq, SAMPLE_CHUNK)
            xs = from_tm(xs_tm)
            s_wkv.append(_unblock_state(sblk))
            s_shift.append(sh[0])
        else:
            nw = _nsa_weights(j, norm_w[i], nsa_w_in, nsa_q_norm, nsa_k_norm, nsa_cmp_pos, nsa_cmp_w1, nsa_cmp_w2,
                              nsa_w_o)
            xp, newp = _nsa_prompt_layer(xp, nw, toep, cbias, bp, tp, tile)
            xs, news = _nsa_sample_layer(j, xs, nw, caches, wins, page_table, sbias, scbias, ts, past)
            wb = min(WINDOW, tp)
            for m_ in range(4):
                p_kv[m_].append(newp[m_].reshape(bp, tp, NSA_KV, NSA_HD))
                s_kv[m_].append(news[m_].reshape(bs, ts, NSA_KV, NSA_HD))
            for m_ in (4, 5):
                p_kv[m_].append(newp[m_].reshape(bp, tp, NSA_KV, NSA_HD)[:, tp - wb:])
                s_kv[m_].append(news[m_].reshape(bs, win_buf, NSA_KV, NSA_HD))
    st = jnp.stack
    return (xp.reshape(bp, tp, d), xs.reshape(bs, ts, d), st(p_wkv), st(p_shift),
            st(p_kv[0]), st(p_kv[1]), st(p_kv[2]), st(p_kv[3]), st(p_kv[4]), st(p_kv[5]),
            st(s_wkv), st(s_shift),
            st(s_kv[0]), st(s_kv[1]), st(s_kv[2]), st(s_kv[3]), st(s_kv[4]), st(s_kv[5]))
```

```python
import functools
import math

import numpy as np
import jax
import jax.numpy as jnp
from jax import lax
from jax.experimental import pallas as pl
from jax.experimental.pallas import tpu as pltpu

F32 = jnp.float32
BF16 = jnp.bfloat16

D_MODEL = 1024
RW_HEAD = 64
RW_HEADS = 16
RW_PAIRS = 8
GN_EPS = 64e-5
RMS_EPS = 1e-6

NSA_HEADS = 16
NSA_KV = 4
NSA_GROUP = 4
NSA_HD = 64
KV_DIM = 256
CMP_BLOCK = 32
CMP_STRIDE = 16
SEL_BLOCK = 64
SEL_TOPN = 16
WINDOW = 512
NUM_BUCKETS = 32
MAX_DISTANCE = 1024
PAGE = 128

LANE = 128
VMEM_LIMIT = 48 * 1024 * 1024

NN = ((1,), (0,))
NT = ((1,), (1,))
TN = ((0,), (0,))
NEG_INF = float("-inf")


def _cp(*sem):
    return pltpu.CompilerParams(dimension_semantics=sem, vmem_limit_bytes=VMEM_LIMIT)


def _dg(a, b, dims=NN):
    return lax.dot_general(a, b, (dims, ((), ())), preferred_element_type=F32)


def _split2(x):
    hi = x.astype(BF16)
    lo = (x - hi.astype(F32)).astype(BF16)
    return hi, lo


def _split3(x):
    x0 = x.astype(BF16)
    r1 = x - x0.astype(F32)
    x1 = r1.astype(BF16)
    x2 = (r1 - x1.astype(F32)).astype(BF16)
    return x0, x1, x2


def _dot3s(a, b, dims=NN):
    return _dg(a[0], b[0], dims) + (_dg(a[0], b[1], dims) + _dg(a[1], b[0], dims))


def _dot3(a, b, dims=NN):
    return _dot3s(_split2(a), _split2(b), dims)


def _wdot(x, wh_ref, wl_ref, passes):
    if passes == 1:
        return _dg(x.astype(BF16), wh_ref[...])
    return _dot3s(_split2(x), (wh_ref[...], wl_ref[...]))


def _dotx(a, e, dims=NN):
    a0, a1, a2 = _split3(a)
    return _dg(a0, e, dims) + (_dg(a1, e, dims) + _dg(a2, e, dims))


def _xdot(e, b, dims=NN):
    b0, b1, b2 = _split3(b)
    return _dg(e, b0, dims) + (_dg(e, b1, dims) + _dg(e, b2, dims))


def _sigmoid(x):
    return 1.0 / (1.0 + jnp.exp(-x))


def _softplus(x):
    return jnp.maximum(x, 0.0) + jnp.log(1.0 + jnp.exp(-jnp.abs(x)))


def _rms_rows(x, w_row):
    ms = jnp.mean(x * x, axis=-1, keepdims=True)
    return x * lax.rsqrt(ms + RMS_EPS) * w_row


def _seg_mats(width):
    r = lax.broadcasted_iota(jnp.int32, (width, LANE), 0)
    c = lax.broadcasted_iota(jnp.int32, (width, LANE), 1)
    g = jnp.where(lax.shift_right_logical(r, 6) == c, 1.0, 0.0).astype(BF16)
    r2 = lax.broadcasted_iota(jnp.int32, (LANE, width), 0)
    c2 = lax.broadcasted_iota(jnp.int32, (LANE, width), 1)
    e = jnp.where(lax.shift_right_logical(c2, 6) == r2, 1.0, 0.0).astype(BF16)
    return g, e


def _segsum(x, g, e):
    return _dotx(_dotx(x, g), e)


def _masked_softmax_rows(logits, valid):
    lf = jnp.where(valid, logits, NEG_INF)
    m = jnp.max(lf, axis=-1, keepdims=True)
    m = jnp.where(m == NEG_INF, 0.0, m)
    ex = jnp.exp(lf - m)
    return ex / jnp.maximum(jnp.sum(ex, axis=-1, keepdims=True), 1e-30)


V_NORM, V_MU0, V_W0, V_A0, V_V0, V_KK, V_KA = 0, 1, 7, 8, 9, 10, 11
MU_R, MU_W, MU_K, MU_V, MU_A, MU_G = 0, 1, 2, 3, 4, 5


def _rwkv_proj_kernel(mode, shift, tiles_per_group, has_vres, *refs):
    if mode == "r":
        (x_ref, init_ref, vec_ref, wh_ref, wl_ref, l1h, l1l, l2h, l2l,
         r_ref, lw_ref, sh_ref, carry) = refs
    elif mode == "k":
        (x_ref, init_ref, vec_ref, wh_ref, wl_ref, l1h, l1l, l2h, l2l,
         k_ref, na_ref, b_ref, carry) = refs
    elif mode == "v":
        if has_vres:
            (x_ref, init_ref, vec_ref, wh_ref, wl_ref, l1h, l1l, l2h, l2l, vf_ref,
             v_ref, carry) = refs
        else:
            (x_ref, init_ref, vec_ref, wh_ref, wl_ref, v_ref, carry) = refs
    else:
        (x_ref, init_ref, vec_ref, wh_ref, wl_ref, l1h, l1l, l2h, l2l,
         gz_ref, carry) = refs

    tt = x_ref.shape[0]
    @pl.when((pl.program_id(0) % tiles_per_group) == 0)
    def _():
        carry[...] = init_ref[...]

    h = _rms_rows(x_ref[...], vec_ref[V_NORM:V_NORM + 1, :])
    prev_rows = carry[...]
    if shift == 1:
        rolled = pltpu.roll(h, 1, axis=0)
        rowi = lax.broadcasted_iota(jnp.int32, (tt, 1), 0)
        prev = jnp.where(rowi == 0, prev_rows, rolled)
    else:
        prev = jnp.concatenate([prev_rows, h[:tt - shift, :]], axis=0)
    carry[...] = h[tt - shift:, :]
    xx = prev - h

    def mix(i):
        return h + xx * vec_ref[V_MU0 + i:V_MU0 + i + 1, :]

    def big(xm):
        return _wdot(xm, wh_ref, wl_ref, 1)

    def lora(xm, act):
        return _wdot(act(_wdot(xm, l1h, l1l, 1)), l2h, l2l, 1)

    if mode == "r":
        r_ref[...] = big(mix(MU_R))
        wraw = vec_ref[V_W0:V_W0 + 1, :] + lora(mix(MU_W), jnp.tanh)
        w = -_softplus(-wraw) - 0.5
        lw_ref[...] = -jnp.exp(w)
        sh_ref[...] = h[tt - shift:, :]
    elif mode == "k":
        kraw = big(mix(MU_K))
        a = _sigmoid(vec_ref[V_A0:V_A0 + 1, :] + lora(mix(MU_A), lambda t: t))
        g, e = _seg_mats(D_MODEL)
        kk = kraw * vec_ref[V_KK:V_KK + 1, :]
        ss = _segsum(kk * kk, g, e)
        kk = kk / jnp.maximum(jnp.sqrt(ss), 1e-12)
        k_ref[...] = kraw * (1.0 + (a - 1.0) * vec_ref[V_KA:V_KA + 1, :])
        na_ref[...] = -kk
        b_ref[...] = kk * a
    elif mode == "v":
        xm = mix(MU_V)
        v = big(xm)
        if has_vres:
            vg = _sigmoid(vec_ref[V_V0:V_V0 + 1, :] + lora(xm, lambda t: t))
            v = v + (vf_ref[...] - v) * vg
        v_ref[...] = v
    else:
        xm = mix(MU_G)
        z = big(xm)
        gate = lora(xm, _sigmoid)
        gz_ref[...] = gate * (z * _sigmoid(z))


def _rwkv_proj(mode, x2d, init_prev, vecs, w_pair, lora_w, vfirst, tile):
    n = x2d.shape[0]
    groups, shift, _ = init_prev.shape
    tiles_per_group = (n // groups) // tile
    has_vres = mode == "v" and lora_w is not None
    row_spec = pl.BlockSpec((tile, D_MODEL), lambda i: (i, 0))
    init_spec = pl.BlockSpec((None, shift, D_MODEL), lambda i: (i // tiles_per_group, 0, 0))
    full = lambda a: pl.BlockSpec(a.shape, lambda i: (0,) * a.ndim)
    ins = [x2d, init_prev, vecs, w_pair[0], w_pair[1]]
    in_specs = [row_spec, init_spec, full(vecs), full(w_pair[0]), full(w_pair[1])]
    if lora_w is not None:
        ins += list(lora_w)
        in_specs += [full(a) for a in lora_w]
    if has_vres:
        ins.append(vfirst)
        in_specs.append(row_spec)
    act = jax.ShapeDtypeStruct((n, D_MODEL), F32)
    if mode == "r":
        out_shape = (act, act, jax.ShapeDtypeStruct(init_prev.shape, F32))
        out_specs = (row_spec, row_spec, init_spec)
    elif mode == "k":
        out_shape = (act, act, act)
        out_specs = (row_spec, row_spec, row_spec)
    else:
        out_shape = act
        out_specs = row_spec
    return pl.pallas_call(
        functools.partial(_rwkv_proj_kernel, mode, shift, tiles_per_group, has_vres),
        grid=(n // tile,),
        in_specs=in_specs,
        out_specs=out_specs,
        out_shape=out_shape,
        scratch_shapes=[pltpu.VMEM((shift, D_MODEL), F32)],
        compiler_params=_cp("arbitrary"),
        name="rwkv_proj_" + mode,
    )(*ins)


def _wkv_pairs(chunk, n_dbl, r, lw, kvec, v, na, bvec, s_old):
    ri = lax.broadcasted_iota(jnp.int32, (chunk, chunk), 0)
    ci = lax.broadcasted_iota(jnp.int32, (chunk, chunk), 1)
    tri = jnp.where(ci <= ri, 1.0, 0.0).astype(BF16)
    idx = range(len(r))
    each = lambda fn: [fn(i) for i in idx]
    cum = each(lambda i: _xdot(tri, lw[i]))
    total = each(lambda i: cum[i][chunk - 1:chunk, :])
    w_in = each(lambda i: jnp.exp(cum[i]))
    w_ex = each(lambda i: jnp.exp(cum[i] - lw[i]))
    w_inv = each(lambda i: jnp.exp(-cum[i]))
    w_rest = each(lambda i: jnp.exp(total[i] - cum[i]))

    lane = lax.broadcasted_iota(jnp.int32, (chunk, LANE), 1)
    m0 = lane < RW_HEAD

    def stack(x):
        return jnp.concatenate([jnp.where(m0, x, 0.0), jnp.where(m0, 0.0, x)], axis=0)

    a_s = each(lambda i: _split2(stack(na[i] * w_ex[i])))
    r_s = each(lambda i: _split2(stack(r[i] * w_in[i])))
    b_s = each(lambda i: _split2(stack(bvec[i] * w_inv[i])))
    k_s = each(lambda i: _split2(stack(kvec[i] * w_inv[i])))
    v_s = each(lambda i: _split2(stack(v[i])))
    bh_s = each(lambda i: _split2(stack(bvec[i] * w_rest[i])))
    kh_s = each(lambda i: _split2(stack(kvec[i] * w_rest[i])))
    s_sp = each(lambda i: _split2(s_old[i]))

    c2 = 2 * chunk
    r2 = lax.broadcasted_iota(jnp.int32, (c2, c2), 0)
    q2 = lax.broadcasted_iota(jnp.int32, (c2, c2), 1)
    strict = q2 < r2
    incl = q2 <= r2
    l_ab = each(lambda i: jnp.where(strict, _dot3s(a_s[i], b_s[i], NT), 0.0))
    l_ak = each(lambda i: jnp.where(strict, _dot3s(a_s[i], k_s[i], NT), 0.0))
    m_rb = each(lambda i: jnp.where(incl, _dot3s(r_s[i], b_s[i], NT), 0.0))
    m_rk = each(lambda i: jnp.where(incl, _dot3s(r_s[i], k_s[i], NT), 0.0))

    x = each(lambda i: _dot3s(a_s[i], s_sp[i], NT) + _dot3s(_split2(l_ak[i]), v_s[i]))
    p = l_ab
    for it in range(n_dbl):
        p_sp = each(lambda i: _split2(p[i]))
        x = each(lambda i: x[i] + _dot3s(p_sp[i], _split2(x[i])))
        if it < n_dbl - 1:
            p = each(lambda i: _dot3s(p_sp[i], p_sp[i]))
    u_s = each(lambda i: _split2(x[i]))
    y = each(lambda i: _dot3s(r_s[i], s_sp[i], NT) + _dot3s(_split2(m_rb[i]), u_s[i])
             + _dot3s(_split2(m_rk[i]), v_s[i]))
    s_new = each(lambda i: s_old[i] * jnp.exp(total[i])
                 + (_dot3s(u_s[i], bh_s[i], TN) + _dot3s(v_s[i], kh_s[i], TN)))
    return each(lambda i: y[i][:chunk, :] + y[i][chunk:, :]), s_new


WKV_PAIRS_PER_STEP = 8


def _wkv_kernel(chunk, n_dbl, r_ref, lw_ref, k_ref, v_ref, na_ref, b_ref, s0_ref,
                y_ref, sout_ref, s_scr):
    c = pl.program_id(2)

    @pl.when(c == 0)
    def _():
        s_scr[...] = s0_ref[...]

    pairs = range(WKV_PAIRS_PER_STEP)
    cols = lambda ref: [ref[:, pp * LANE:(pp + 1) * LANE] for pp in pairs]
    y, s_new = _wkv_pairs(chunk, n_dbl, cols(r_ref), cols(lw_ref), cols(k_ref), cols(v_ref), cols(na_ref),
                          cols(b_ref), [s_scr[pp] for pp in pairs])
    for pp in pairs:
        y_ref[:, pp * LANE:(pp + 1) * LANE] = y[pp]
        s_scr[pp] = s_new[pp]

    @pl.when(c == pl.num_programs(2) - 1)
    def _():
        sout_ref[...] = s_scr[...]


def _wkv(r, lw, k, v, na, b, s0blk, chunk):
    g, t, _ = r.shape
    n_dbl = int(round(math.log2(chunk)))
    pps = WKV_PAIRS_PER_STEP
    seq_spec = pl.BlockSpec((None, chunk, pps * LANE), lambda gi, p, c: (gi, c, p))
    st_spec = pl.BlockSpec((None, pps, LANE, LANE), lambda gi, p, c: (gi, p, 0, 0))
    return pl.pallas_call(
        functools.partial(_wkv_kernel, chunk, n_dbl),
        grid=(g, RW_PAIRS // pps, t // chunk),
        in_specs=[seq_spec] * 6 + [st_spec],
        out_specs=(seq_spec, st_spec),
        out_shape=(jax.ShapeDtypeStruct(r.shape, F32), jax.ShapeDtypeStruct(s0blk.shape, F32)),
        scratch_shapes=[pltpu.VMEM((pps, LANE, LANE), F32)],
        compiler_params=_cp("arbitrary", "arbitrary", "arbitrary"),
        name="wkv_scan",
    )(r, lw, k, v, na, b, s0blk)


def _rwkv_out_kernel(y_ref, r_ref, k_ref, v_ref, gz_ref, x_ref, vec_ref, wh_ref, wl_ref, o_ref):
    g, e = _seg_mats(D_MODEL)
    y = y_ref[...]
    inv = 1.0 / RW_HEAD
    mean = _segsum(y, g, e) * inv
    d = y - mean
    var = _segsum(d * d, g, e) * inv
    yn = d * lax.rsqrt(var + GN_EPS) * vec_ref[1:2, :] + vec_ref[2:3, :]
    v = v_ref[...]
    bonus = _segsum(r_ref[...] * k_ref[...] * vec_ref[0:1, :], g, e) * v
    o = (yn + bonus) * gz_ref[...]
    o_ref[...] = x_ref[...] + _wdot(o, wh_ref, wl_ref, 1)


def _rwkv_out(y, r, k, v, gz, x2d, vecs, w_pair, tile):
    n = x2d.shape[0]
    row_spec = pl.BlockSpec((tile, D_MODEL), lambda i: (i, 0))
    full = lambda a: pl.BlockSpec(a.shape, lambda i: (0,) * a.ndim)
    return pl.pallas_call(
        _rwkv_out_kernel,
        grid=(n // tile,),
        in_specs=[row_spec] * 6 + [full(vecs), full(w_pair[0]), full(w_pair[1])],
        out_specs=row_spec,
        out_shape=jax.ShapeDtypeStruct((n, D_MODEL), F32),
        compiler_params=_cp("arbitrary"),
        name="rwkv_out",
    )(y, r, k, v, gz, x2d, vecs, w_pair[0], w_pair[1])


def _wsplit(w):
    hi = w.astype(BF16)
    lo = (w - hi.astype(F32)).astype(BF16)
    return hi, lo


def _pad_to(a, axis, size):
    pad = [(0, 0)] * a.ndim
    pad[axis] = (0, size - a.shape[axis])
    return jnp.pad(a, pad)


def _lora_pair(w1, w2):
    rank = w1.shape[1]
    rp = ((rank + LANE - 1) // LANE) * LANE
    return _wsplit(_pad_to(w1, 1, rp)) + _wsplit(_pad_to(w2, 0, rp))


def _blockdiag_state(s):
    g = s.shape[0]
    s = s.reshape(g, RW_PAIRS, 2, RW_HEAD, RW_HEAD)
    z = jnp.zeros_like(s[:, :, 0])
    top = jnp.concatenate([s[:, :, 0], z], axis=-1)
    bot = jnp.concatenate([z, s[:, :, 1]], axis=-1)
    return jnp.concatenate([top, bot], axis=-2)


def _unblock_state(sb):
    g = sb.shape[0]
    a = sb[:, :, :RW_HEAD, :RW_HEAD]
    b = sb[:, :, RW_HEAD:, RW_HEAD:]
    return jnp.stack([a, b], axis=2).reshape(g, RW_HEADS, RW_HEAD, RW_HEAD)


def _rwkv_layer(x2d, init_prev, s0blk, vfirst, lw, tile, to_seq, from_seq, chunk):
    vecs, w_r, w_k, w_v, w_z, lo_w, lo_a, lo_v, lo_g, vecs_out, w_o = lw
    r, logw, shift_out = _rwkv_proj("r", x2d, init_prev, vecs, w_r, lo_w, None, tile)
    k, na, b = _rwkv_proj("k", x2d, init_prev, vecs, w_k, lo_a, None, tile)
    v = _rwkv_proj("v", x2d, init_prev, vecs, w_v, lo_v, vfirst, tile)
    gz = _rwkv_proj("z", x2d, init_prev, vecs, w_z, lo_g, None, tile)
    if vfirst is None:
        vfirst = v
    y, sblk = _wkv(*(to_seq(t) for t in (r, logw, k, v, na, b)), s0blk, chunk)
    x_new = _rwkv_out(from_seq(y), r, k, v, gz, x2d, vecs_out, w_o, tile)
    return x_new, shift_out, sblk, vfirst


def _rwkv_weights(j, norm_w_i, rw_mu, rw_w_rkvz, rw_w0, rw_w1, rw_w2, rw_a0, rw_a1, rw_a2, rw_v0, rw_v1,
                  rw_v2, rw_g1, rw_g2, rw_k_k, rw_k_a, rw_r_k, rw_ln_w, rw_ln_b, rw_w_o):
    zero = jnp.zeros((D_MODEL,), F32)
    v0 = rw_v0[j - 1] if j > 0 else zero
    vecs = jnp.stack([norm_w_i] + [rw_mu[j, i] for i in range(6)]
                     + [rw_w0[j], rw_a0[j], v0, rw_k_k[j], rw_k_a[j]] + [zero] * 4)
    lo_v = _lora_pair(rw_v1[j - 1], rw_v2[j - 1]) if j > 0 else None
    vecs_out = jnp.stack([rw_r_k[j].reshape(D_MODEL), rw_ln_w[j], rw_ln_b[j]] + [zero] * 5)
    return (vecs, _wsplit(rw_w_rkvz[j, 0]), _wsplit(rw_w_rkvz[j, 1]), _wsplit(rw_w_rkvz[j, 2]),
            _wsplit(rw_w_rkvz[j, 3]), _lora_pair(rw_w1[j], rw_w2[j]), _lora_pair(rw_a1[j], rw_a2[j]),
            lo_v, _lora_pair(rw_g1[j], rw_g2[j]), vecs_out, _wsplit(rw_w_o[j]))


def _seg_rms(y, aux_row, width):
    g, e = _seg_mats(width)
    ms = _segsum(y * y, g, e) * (1.0 / NSA_HD)
    return y * lax.rsqrt(ms + RMS_EPS) * aux_row


def _nsa_proj_kernel(mode, x_ref, nw_ref, wh_ref, wl_ref, aux_ref, *out_refs):
    h = _rms_rows(x_ref[...], nw_ref[...])
    y = _wdot(h, wh_ref, wl_ref, 3 if mode in ("q", "kv") else 1)
    if mode == "q":
        out_refs[0][...] = _seg_rms(y, aux_ref[0:1, :], D_MODEL) * (NSA_HD ** -0.5)
    elif mode == "kv":
        kc, vc, ks, vs, kw, vw = (y[:, i * KV_DIM:(i + 1) * KV_DIM] for i in range(6))
        out_refs[0][...] = kc
        out_refs[1][...] = vc
        out_refs[2][...] = _seg_rms(ks, aux_ref[0:1, :], KV_DIM)
        out_refs[3][...] = vs
        out_refs[4][...] = _seg_rms(kw, aux_ref[1:2, :], KV_DIM)
        out_refs[5][...] = vw
    elif mode == "z":
        out_refs[0][...] = y * _sigmoid(y)
    else:
        out_refs[0][...] = _sigmoid(y)


def _nsa_proj(mode, x2d, nw_row, w_pair, aux, tile):
    n = x2d.shape[0]
    width = w_pair[0].shape[1]
    row_spec = pl.BlockSpec((tile, D_MODEL), lambda i: (i, 0))
    full = lambda a: pl.BlockSpec(a.shape, lambda i: (0,) * a.ndim)
    if mode == "kv":
        out_shape = tuple(jax.ShapeDtypeStruct((n, KV_DIM), F32) for _ in range(6))
        out_specs = tuple(pl.BlockSpec((tile, KV_DIM), lambda i: (i, 0)) for _ in range(6))
    else:
        out_shape = (jax.ShapeDtypeStruct((n, width), F32),)
        out_specs = (pl.BlockSpec((tile, width), lambda i: (i, 0)),)
    res = pl.pallas_call(
        functools.partial(_nsa_proj_kernel, mode),
        grid=(n // tile,),
        in_specs=[row_spec, full(nw_row), full(w_pair[0]), full(w_pair[1]), full(aux)],
        out_specs=out_specs,
        out_shape=out_shape,
        compiler_params=_cp("arbitrary"),
        name="nsa_proj_" + mode,
    )(x2d, nw_row, w_pair[0], w_pair[1], aux)
    return res if mode == "kv" else res[0]


def _compress_kernel(n_prefetch, n_src, m, do_norm, *refs):
    refs = refs[n_prefetch:]
    src_lo = refs[:n_src]
    src_hi = refs[n_src:2 * n_src]
    pos_ref, wah, wal, wbh, wbl, w2h, w2l, nrm_ref, o_ref = refs[2 * n_src:]
    rows_per_src = m // n_src
    acc_a = jnp.zeros((m, KV_DIM), F32)
    acc_b = jnp.zeros((m, KV_DIM), F32)

    def strided(srcs, jj):
        parts = [s[pl.ds(jj, rows_per_src, stride=CMP_STRIDE), :] for s in srcs]
        return parts[0] if n_src == 1 else jnp.concatenate(parts, axis=0)

    for jj in range(CMP_STRIDE):
        x = jnp.concatenate([strided(src_lo, jj), strided(src_hi, jj)], axis=-1)
        xa = _split2(x + pos_ref[jj:jj + 1, :])
        xb = _split2(x + pos_ref[CMP_STRIDE + jj:CMP_STRIDE + jj + 1, :])
        acc_a = acc_a + _dot3s(xa, (wah[jj], wal[jj]))
        acc_b = acc_b + _dot3s(xb, (wbh[jj], wbl[jj]))
    pre = acc_a + pltpu.roll(acc_b, m - 1, axis=0)
    hid = pre * _sigmoid(pre)
    out = _dot3s(_split2(hid), (w2h[...], w2l[...]))
    if do_norm:
        out = _seg_rms(out, nrm_ref[0:1, :], KV_DIM)
    o_ref[...] = out


def _compress_call(srcs, half_specs, grid, n_prefetch, prefetch, cw, do_norm, m, batch, idx):
    pos, wa, wb, w2, nrm = cw
    consts = [pos, wa[0], wa[1], wb[0], wb[1], w2[0], w2[1], nrm]
    cspecs = [pl.BlockSpec(a.shape, functools.partial(lambda nd, *_: (0,) * nd, a.ndim)) for a in consts]
    out_spec = pl.BlockSpec((None, m, KV_DIM), idx)
    gs = pltpu.PrefetchScalarGridSpec(
        num_scalar_prefetch=n_prefetch, grid=grid, in_specs=half_specs(0) + half_specs(1) + cspecs,
        out_specs=out_spec)
    return pl.pallas_call(
        functools.partial(_compress_kernel, n_prefetch, len(srcs), m, do_norm),
        grid_spec=gs,
        out_shape=jax.ShapeDtypeStruct((batch, m, KV_DIM), F32),
        compiler_params=_cp("arbitrary"),
        name="nsa_compress",
    )(*prefetch, *srcs, *srcs, *consts)


def _compress_weights(pos, w1, w2, knorm_row):
    eye = jnp.eye(NSA_KV, dtype=F32)
    w1r = w1.reshape(CMP_BLOCK, NSA_HD, NSA_HD)
    blk = jnp.einsum("ab,jde->jadbe", eye, w1r).reshape(CMP_BLOCK, KV_DIM, KV_DIM)
    w2b = jnp.einsum("ab,de->adbe", eye, w2).reshape(KV_DIM, KV_DIM)
    nrm = jnp.zeros((8, KV_DIM), F32).at[0].set(jnp.tile(knorm_row, NSA_KV))
    return (jnp.tile(pos, (1, NSA_KV)), _wsplit(blk[:CMP_STRIDE]), _wsplit(blk[CMP_STRIDE:]), _wsplit(w2b), nrm)


def _bucket_thresholds():
    n = np.arange(0, 8192, dtype=np.float64)
    max_exact = NUM_BUCKETS // 2
    large = max_exact + np.floor(
        np.log(np.maximum(n, 1.0) / max_exact) / math.log(MAX_DISTANCE / max_exact) * (NUM_BUCKETS - max_exact))
    bucket = np.where(n < max_exact, n, np.minimum(large, NUM_BUCKETS - 1)).astype(np.int64)
    return [int(np.argmax(bucket >= m)) for m in range(NUM_BUCKETS)]


def _bias_kernel(thr, base_mul, base_add, col_stride, tab_ref, o_ref):
    i = pl.program_id(0)
    h = pl.program_id(1)
    rows, cols = o_ref.shape
    r = lax.broadcasted_iota(jnp.int32, (rows, cols), 0)
    c = lax.broadcasted_iota(jnp.int32, (rows, cols), 1)
    d = (i * base_mul + base_add) + r - c * col_stride
    val = jnp.full((rows, cols), tab_ref[0, h], F32)
    for m in range(1, NUM_BUCKETS):
        val = jnp.where(d >= thr[m], tab_ref[m, h], val)
    o_ref[...] = val


def _bias_table(rel_bias, n_i, rows, cols, base_mul, base_add, col_stride):
    return pl.pallas_call(
        functools.partial(_bias_kernel, _bucket_thresholds(), base_mul, base_add, col_stride),
        grid=(n_i, NSA_HEADS),
        in_specs=[pl.BlockSpec(memory_space=pltpu.SMEM)],
        out_specs=pl.BlockSpec((None, None, rows, cols), lambda i, h: (i, h, 0, 0)),
        out_shape=jax.ShapeDtypeStruct((n_i, NSA_HEADS, rows, cols), F32),
        compiler_params=_cp("arbitrary", "arbitrary"),
        name="nsa_bias_table",
    )(rel_bias)


def _overlap_t(n_cmp, n_cmp_pad, n_sel, n_sel_pad):
    cs = np.arange(n_cmp_pad)[None, :] * CMP_STRIDE
    ss = np.arange(n_sel_pad)[:, None] * SEL_BLOCK
    ov = np.maximum(np.minimum(cs + CMP_BLOCK, ss + SEL_BLOCK) - np.maximum(cs, ss), 0) / CMP_BLOCK
    ov = ov * (np.arange(n_cmp_pad)[None, :] < n_cmp) * (np.arange(n_sel_pad)[:, None] < n_sel)
    return jnp.asarray(ov, F32)


def _online_update(carry, q_sp, k_tiles, v_tiles, add_tiles, qk_passes=1):
    m, l, acc = carry
    if qk_passes == 1:
        lfs = [_dg(q_sp[0], k.astype(BF16), NT) + a for k, a in zip(k_tiles, add_tiles)]
    else:
        lfs = [_dot3s(q_sp, _split2(k), NT) + a for k, a in zip(k_tiles, add_tiles)]
    emax = functools.reduce(jnp.maximum, lfs)
    m_new = jnp.maximum(m, jnp.max(emax, axis=-1, keepdims=True))
    m_safe = jnp.where(m_new == NEG_INF, 0.0, m_new)
    alpha = jnp.exp(m - m_safe)
    ps = [jnp.exp(lf - m_safe) for lf in lfs]
    l = alpha * l + jnp.sum(functools.reduce(jnp.add, ps), axis=-1, keepdims=True)
    pv = functools.reduce(jnp.add, [_dg(p.astype(BF16), v.astype(BF16)) for p, v in zip(ps, v_tiles)])
    return m_new, l, alpha * acc + pv


def _attn_init(rows):
    return (jnp.full((rows, 1), NEG_INF, F32), jnp.zeros((rows, 1), F32), jnp.zeros((rows, LANE), F32))


def _attn_finish(carry):
    _, l, acc = carry
    return acc / jnp.maximum(l, 1e-30)


def _select_blocks(imp_t, qpos_lane, sc_scr):
    n_sel = imp_t.shape[0]
    sidx = lax.broadcasted_iota(jnp.int32, imp_t.shape, 0)
    cur = lax.shift_right_logical(qpos_lane, 6)
    future = sidx * SEL_BLOCK > qpos_lane
    forced = (sidx == 0) | (sidx == cur) | (sidx == cur - 1)
    score = jnp.where(future, NEG_INF, jnp.where(forced, float("inf"), imp_t))
    sc_scr[...] = score

    def body(s2, cnt):
        rowv = sc_scr[pl.ds(s2, 1), :]
        ahead = (rowv > score) | ((rowv == score) & (s2 < sidx))
        return cnt + jnp.where(ahead, 1.0, 0.0)

    cnt = lax.fori_loop(0, n_sel, body, jnp.zeros(imp_t.shape, F32), unroll=4)
    return jnp.where(cnt < SEL_TOPN, 1.0, 0.0).astype(BF16)


def _block_mask(sel_t, kt):
    n_sel = sel_t.shape[0]
    sr = lax.broadcasted_iota(jnp.int32, (n_sel, LANE), 0)
    kc = lax.broadcasted_iota(jnp.int32, (n_sel, LANE), 1)
    expand = jnp.where(sr == 2 * kt + lax.shift_right_logical(kc, 6), 1.0, 0.0).astype(BF16)
    return _dg(sel_t, expand, TN)


def _lane_mask(par, rows):
    lane = lax.broadcasted_iota(jnp.int32, (rows, LANE), 1)
    lo = par * NSA_HD
    return (lane >= lo) & (lane < lo + NSA_HD)


SEL_GROUP = 4


def _nsa_prompt_kernel(n_sel, mc, *refs):
    q_refs = refs[0:4]
    gate_ref, kc_ref, vc_ref, ks_ref, vs_ref = refs[4:9]
    kw_refs = refs[9:14]
    vw_refs = refs[14:19]
    toep_ref, cb_ref, ovt_ref = refs[19:22]
    o_refs = refs[22:26]
    sc_scr = refs[26]

    qb = pl.program_id(1)
    par = pl.program_id(2) % 2
    mk = _lane_mask(par, LANE)
    q = jnp.concatenate([jnp.where(mk, r[...], 0.0) for r in q_refs], axis=0)
    q_sp = _split2(q)
    rows = NSA_GROUP * LANE
    qpos = qb * LANE + (lax.broadcasted_iota(jnp.int32, (rows, 1), 0) & (LANE - 1))

    cend = lax.broadcasted_iota(jnp.int32, (1, mc), 1) * CMP_STRIDE + (CMP_BLOCK - 1)
    lc = _dot3s(q_sp, _split2(kc_ref[...]), NT) + jnp.concatenate([cb_ref[g] for g in range(NSA_GROUP)], axis=0)
    p_c = _masked_softmax_rows(lc, cend <= qpos)
    o_c = _dot3(p_c, vc_ref[...])
    psum = (p_c[0:LANE] + p_c[LANE:2 * LANE]) + (p_c[2 * LANE:3 * LANE] + p_c[3 * LANE:4 * LANE])
    imp_t = _dot3(ovt_ref[...], psum, NT)
    qpos_lane = qb * LANE + lax.broadcasted_iota(jnp.int32, (n_sel, LANE), 1)
    sel_t = _select_blocks(imp_t, qpos_lane, sc_scr)

    qrow = qb * LANE + lax.broadcasted_iota(jnp.int32, (LANE, LANE), 0)
    kcol = lax.broadcasted_iota(jnp.int32, (LANE, LANE), 1)

    def sel_body(it, carry):
        k_tiles, v_tiles, adds = [], [], []
        for i in range(SEL_GROUP):
            kt = it * SEL_GROUP + i
            start = pl.multiple_of(kt * LANE, LANE)
            ok = (_block_mask(sel_t, kt) > 0.5) & (kt * LANE + kcol <= qrow)
            madd = jnp.where(ok, 0.0, NEG_INF)
            ti = jnp.clip(qb - kt, 0, 8)
            adds.append(jnp.concatenate([toep_ref[ti, g] + madd for g in range(NSA_GROUP)], axis=0))
            k_tiles.append(ks_ref[pl.ds(start, LANE), :])
            v_tiles.append(vs_ref[pl.ds(start, LANE), :])
        return _online_update(carry, q_sp, k_tiles, v_tiles, adds)

    n_it = (qb + SEL_GROUP) // SEL_GROUP
    o_s = _attn_finish(lax.fori_loop(0, n_it, sel_body, _attn_init(rows)))

    adds = []
    for i in range(5):
        kpos = (qb - 4 + i) * LANE + kcol
        dist = qrow - kpos
        ok = (kpos >= 0) & (dist >= 0) & (dist < WINDOW)
        madd = jnp.where(ok, 0.0, NEG_INF)
        adds.append(jnp.concatenate([toep_ref[4 - i, g] + madd for g in range(NSA_GROUP)], axis=0))
    o_w = _attn_finish(_online_update(_attn_init(rows), q_sp, [r[...] for r in kw_refs], [r[...] for r in vw_refs],
                                      adds))

    for g in range(NSA_GROUP):
        sl = slice(g * LANE, (g + 1) * LANE)
        og = (gate_ref[:, 3 * g:3 * g + 1] * o_c[sl] + gate_ref[:, 3 * g + 1:3 * g + 2] * o_s[sl]
              + gate_ref[:, 3 * g + 2:3 * g + 3] * o_w[sl])
        og = jnp.where(mk, og, 0.0)
        o_ref = o_refs[g]

        @pl.when(par == 0)
        def _():
            o_ref[...] = og

        @pl.when(par == 1)
        def _():
            o_ref[...] = o_ref[...] + og


def _nsa_prompt_attn(qp, gates, kcmp, vcmp, ks, vs, kw, vw, toep, cbias, batch, seq):
    nqb = seq // LANE
    mc = seq // CMP_STRIDE
    n_sel = seq // SEL_BLOCK
    ovt = _overlap_t(mc - 1, mc, n_sel, n_sel)
    q_specs = [pl.BlockSpec((LANE, LANE), functools.partial(lambda g, b, qb, kh: (b * nqb + qb, 2 * g + kh // 2), g))
               for g in range(NSA_GROUP)]
    half = lambda rows: pl.BlockSpec((None, rows, LANE), lambda b, qb, kh: (b, 0, kh // 2))
    win_specs = [pl.BlockSpec((None, LANE, LANE),
                              functools.partial(lambda i, b, qb, kh: (b, jnp.maximum(qb - 4 + i, 0), kh // 2), i))
                 for i in range(5)]
    in_specs = (q_specs
                + [pl.BlockSpec((LANE, LANE), lambda b, qb, kh: (b * nqb + qb, kh))]
                + [half(mc), half(mc), half(seq), half(seq)]
                + win_specs + win_specs
                + [pl.BlockSpec((9, NSA_GROUP, LANE, LANE), lambda b, qb, kh: (0, kh, 0, 0)),
                   pl.BlockSpec((None, NSA_GROUP, LANE, mc), lambda b, qb, kh: (qb, kh, 0, 0)),
                   pl.BlockSpec(ovt.shape, lambda b, qb, kh: (0, 0))])
    out_spec = pl.BlockSpec((LANE, LANE), lambda b, qb, kh: (b * nqb + qb, kh // 2))
    n = batch * seq
    return pl.pallas_call(
        functools.partial(_nsa_prompt_kernel, n_sel, mc),
        grid=(batch, nqb, NSA_KV),
        in_specs=in_specs,
        out_specs=(out_spec,) * NSA_GROUP,
        out_shape=tuple(jax.ShapeDtypeStruct((n, KV_DIM), F32) for _ in range(NSA_GROUP)),
        scratch_shapes=[pltpu.VMEM((n_sel, LANE), F32)],
        compiler_params=_cp("arbitrary", "arbitrary", "arbitrary"),
        name="nsa_prompt_attn",
    )(qp, qp, qp, qp, gates, kcmp, vcmp, ks, vs, *([kw] * 5), *([vw] * 5), toep, cbias, ovt)


T_PAD = 8
SAMPLE_GROUP = 8


def _nsa_sample_kernel(past, dec_seq, n_pages, win_buf, *refs):
    n = n_pages
    t_new = dec_seq
    q_refs = refs[1:5]
    gate_ref, kc_ref, vc_ref = refs[5:8]
    ks_pages = refs[8:8 + n]
    vs_pages = refs[8 + n:8 + 2 * n]
    ksn, vsn, kwn, vwn, wk_ref, wv_ref, sb_ref, cb_ref, ovt_ref = refs[8 + 2 * n:17 + 2 * n]
    o_refs = refs[17 + 2 * n:21 + 2 * n]
    wko_ref, wvo_ref, sc_scr = refs[21 + 2 * n:24 + 2 * n]

    par = pl.program_id(1) % 2
    rows = NSA_GROUP * T_PAD
    mk8 = _lane_mask(par, T_PAD)
    q = jnp.concatenate([jnp.where(mk8, r[...], 0.0) for r in q_refs], axis=0)
    q_sp = _split2(q)
    qpos = past + (lax.broadcasted_iota(jnp.int32, (rows, 1), 0) & (T_PAD - 1))
    qpos8 = past + lax.broadcasted_iota(jnp.int32, (T_PAD, LANE), 0)
    kcol = lax.broadcasted_iota(jnp.int32, (T_PAD, LANE), 1)
    zpad = jnp.zeros((LANE - T_PAD, LANE), F32)

    def per_group(fn):
        return jnp.concatenate([fn(g) for g in range(NSA_GROUP)], axis=0)

    mc = kc_ref.shape[0]
    cend = lax.broadcasted_iota(jnp.int32, (1, mc), 1) * CMP_STRIDE + (CMP_BLOCK - 1)
    lc = _dot3s(q_sp, _split2(kc_ref[...]), NT) + per_group(lambda g: cb_ref[g])
    p_c = _masked_softmax_rows(lc, cend <= qpos)
    o_c = _dot3(p_c, vc_ref[...])
    psum = (p_c[0:T_PAD] + p_c[T_PAD:2 * T_PAD]) + (p_c[2 * T_PAD:3 * T_PAD] + p_c[3 * T_PAD:4 * T_PAD])
    imp_t = _dot3(ovt_ref[...], jnp.concatenate([psum, zpad], axis=0), NT)
    qpos_lane = past + lax.broadcasted_iota(jnp.int32, imp_t.shape, 1)
    sel_t = _select_blocks(imp_t, qpos_lane, sc_scr)

    def tile_add(kt, base, extra_ok):
        kpos = base + kcol
        ok = (kpos <= qpos8) & extra_ok(kpos)
        if kt is not None:
            ok = ok & (_block_mask(sel_t, kt)[:T_PAD, :] > 0.5)
        madd = jnp.where(ok, 0.0, NEG_INF)
        return per_group(lambda g: sb_ref[g, :, base:base + LANE] + madd)

    always = lambda kpos: kpos >= 0
    k_new = jnp.concatenate([ksn[...], zpad], axis=0)
    v_new = jnp.concatenate([vsn[...], zpad], axis=0)
    carry = _attn_init(rows)
    for p0 in range(0, n, SAMPLE_GROUP):
        ps = range(p0, min(p0 + SAMPLE_GROUP, n))
        carry = _online_update(carry, q_sp, [ks_pages[p][...] for p in ps], [vs_pages[p][...] for p in ps],
                               [tile_add(p, p * PAGE, always) for p in ps], qk_passes=3)
    o_s = _attn_finish(_online_update(carry, q_sp, [k_new], [v_new], [tile_add(n, past, always)], qk_passes=3))

    in_win = lambda kpos: (qpos8 - kpos) < WINDOW
    w0 = past - win_buf
    n_wt = win_buf // LANE
    kw_new = jnp.concatenate([kwn[...], zpad], axis=0)
    vw_new = jnp.concatenate([vwn[...], zpad], axis=0)
    o_w = _attn_finish(_online_update(
        _attn_init(rows), q_sp,
        [wk_ref[i * LANE:(i + 1) * LANE, :] for i in range(n_wt)] + [kw_new],
        [wv_ref[i * LANE:(i + 1) * LANE, :] for i in range(n_wt)] + [vw_new],
        [tile_add(None, w0 + i * LANE, in_win) for i in range(n_wt)] + [tile_add(None, past, in_win)], qk_passes=3))

    for g in range(NSA_GROUP):
        sl = slice(g * T_PAD, (g + 1) * T_PAD)
        og = (gate_ref[:, 3 * g:3 * g + 1] * o_c[sl] + gate_ref[:, 3 * g + 1:3 * g + 2] * o_s[sl]
              + gate_ref[:, 3 * g + 2:3 * g + 3] * o_w[sl])
        og = jnp.where(mk8, og, 0.0)
        o_ref = o_refs[g]

        @pl.when(par == 0)
        def _():
            o_ref[...] = og

        @pl.when(par == 1)
        def _():
            o_ref[...] = o_ref[...] + og

    rowi = lax.broadcasted_iota(jnp.int32, (LANE, LANE), 0)
    for src, new, dst in ((wk_ref, kw_new, wko_ref), (wv_ref, vw_new, wvo_ref)):
        shifted = pltpu.roll(src[...], win_buf - t_new, axis=0)
        tail = pltpu.roll(new, LANE - t_new, axis=0)
        dst[0:win_buf - LANE, :] = shifted[0:win_buf - LANE, :]
        dst[win_buf - LANE:win_buf, :] = jnp.where(rowi >= LANE - t_new, tail, shifted[win_buf - LANE:win_buf, :])


def _nsa_sample_attn(j, page_table, qp8, gates8, kcmp, vcmp, cache_k, cache_v, new8, win_k, win_v, sbias, scbias,
                     past, dec_seq):
    batch, n_pages = page_table.shape
    win_buf = win_k.shape[2]
    mc = kcmp.shape[1]
    n_sel = -(-(past + dec_seq) // SEL_BLOCK)
    n_sel_pad = -(-n_sel // 8) * 8
    ovt = _overlap_t(mc - 1, mc, n_sel, n_sel_pad)
    q_specs = [pl.BlockSpec((None, T_PAD, LANE), functools.partial(lambda g, b, kh, pt: (b, 0, 2 * g + kh // 2), g))
               for g in range(NSA_GROUP)]
    half = lambda rows: pl.BlockSpec((None, rows, LANE), lambda b, kh, pt: (b, 0, kh // 2))
    page_specs = [pl.BlockSpec((None, None, PAGE, LANE),
                               functools.partial(lambda p, b, kh, pt: (j, pt[b, p], 0, kh // 2), p))
                  for p in range(n_pages)]
    win_spec = pl.BlockSpec((None, None, win_buf, LANE), lambda b, kh, pt: (j, b, 0, kh // 2))
    in_specs = (q_specs
                + [pl.BlockSpec((None, T_PAD, LANE), lambda b, kh, pt: (b, 0, kh)), half(mc), half(mc)]
                + page_specs + page_specs
                + [half(T_PAD)] * 4 + [win_spec, win_spec]
                + [pl.BlockSpec((NSA_GROUP, T_PAD, sbias.shape[2]), lambda b, kh, pt: (kh, 0, 0)),
                   pl.BlockSpec((NSA_GROUP, T_PAD, mc), lambda b, kh, pt: (kh, 0, 0)),
                   pl.BlockSpec(ovt.shape, lambda b, kh, pt: (0, 0))])
    out_specs = (half(T_PAD),) * NSA_GROUP + (half(win_buf), half(win_buf))
    out_shape = (tuple(jax.ShapeDtypeStruct((batch, T_PAD, KV_DIM), F32) for _ in range(NSA_GROUP))
                 + tuple(jax.ShapeDtypeStruct((batch, win_buf, KV_DIM), F32) for _ in range(2)))
    gs = pltpu.PrefetchScalarGridSpec(
        num_scalar_prefetch=1, grid=(batch, NSA_KV), in_specs=in_specs, out_specs=out_specs,
        scratch_shapes=[pltpu.VMEM((n_sel_pad, LANE), F32)])
    res = pl.pallas_call(
        functools.partial(_nsa_sample_kernel, past, dec_seq, n_pages, win_buf),
        grid_spec=gs,
        out_shape=out_shape,
        compiler_params=_cp("arbitrary", "arbitrary"),
        name="nsa_sample_attn",
    )(page_table, qp8, qp8, qp8, qp8, gates8, kcmp, vcmp, *([cache_k] * n_pages), *([cache_v] * n_pages),
      *new8, win_k, win_v, sbias, scbias, ovt)
    return res[:NSA_GROUP], res[NSA_GROUP], res[NSA_GROUP + 1]


def _nsa_sample_layer(j, x2d, nw, caches, wins, page_table, sbias, scbias, dec_seq, past):
    batch, n_pages = page_table.shape
    n = x2d.shape[0]
    qp, kvs, sz, gates = _nsa_project_all(x2d, nw, n)
    cw_k, cw_v, w_o = nw[7:]
    m = n_pages * PAGE // CMP_STRIDE
    page_specs = lambda h: [pl.BlockSpec((None, None, PAGE, LANE),
                                         functools.partial(lambda p, b, pt: (j, pt[b, p], 0, h), p))
                            for p in range(n_pages)]
    out_idx = lambda b, pt: (b, 0, 0)
    kcmp = _compress_call([caches[0]] * n_pages, page_specs, (batch,), 1, (page_table,), cw_k, True, m, batch, out_idx)
    vcmp = _compress_call([caches[1]] * n_pages, page_specs, (batch,), 1, (page_table,), cw_v, False, m, batch, out_idx)
    pad8 = lambda a: _pad_to(a.reshape(batch, dec_seq, a.shape[-1]), 1, T_PAD)
    new8 = tuple(pad8(kvs[i]) for i in (2, 3, 4, 5))
    o8, win_k_new, win_v_new = _nsa_sample_attn(j, page_table, pad8(qp), pad8(gates), kcmp, vcmp, caches[2], caches[3],
                                                new8, wins[0], wins[1], sbias, scbias, past, dec_seq)
    o_parts = [o[:, :dec_seq].reshape(n, KV_DIM) for o in o8]
    x_new = _nsa_out(o_parts, sz, x2d, w_o, n)
    return x_new, kvs[:4] + (win_k_new, win_v_new)


def _nsa_out_kernel(o0, o1, o2, o3, sz_ref, x_ref, wh_ref, wl_ref, out_ref):
    o = jnp.concatenate([o0[...], o1[...], o2[...], o3[...]], axis=-1) * sz_ref[...]
    out_ref[...] = x_ref[...] + _wdot(o, wh_ref, wl_ref, 1)


def _nsa_out(o_parts, sz, x2d, w_pair, tile):
    n = x2d.shape[0]
    row_spec = pl.BlockSpec((tile, D_MODEL), lambda i: (i, 0))
    part_spec = pl.BlockSpec((tile, KV_DIM), lambda i: (i, 0))
    full = lambda a: pl.BlockSpec(a.shape, lambda i: (0,) * a.ndim)
    return pl.pallas_call(
        _nsa_out_kernel,
        grid=(n // tile,),
        in_specs=[part_spec] * 4 + [row_spec, row_spec, full(w_pair[0]), full(w_pair[1])],
        out_specs=row_spec,
        out_shape=jax.ShapeDtypeStruct((n, D_MODEL), F32),
        compiler_params=_cp("arbitrary"),
        name="nsa_out",
    )(*o_parts, sz, x2d, w_pair[0], w_pair[1])


def _nsa_weights(j, norm_w_i, nsa_w_in, nsa_q_norm, nsa_k_norm, nsa_cmp_pos, nsa_cmp_w1, nsa_cmp_w2, nsa_w_o):
    w_in = nsa_w_in[j]
    qd = NSA_HEADS * NSA_HD

    def perm_cols(w):
        return w.reshape(D_MODEL, NSA_KV, NSA_GROUP, NSA_HD).transpose(0, 2, 1, 3).reshape(D_MODEL, qd)

    w_q = perm_cols(w_in[:, :qd])
    w_kv = w_in[:, qd:qd + 6 * KV_DIM]
    w_z = perm_cols(w_in[:, qd + 6 * KV_DIM:2 * qd + 6 * KV_DIM])
    w_g = w_in[:, 2 * qd + 6 * KV_DIM:].reshape(D_MODEL, NSA_KV, 3 * NSA_GROUP)
    w_g = _pad_to(w_g, 2, LANE).reshape(D_MODEL, NSA_KV * LANE)
    w_o = nsa_w_o[j].reshape(NSA_KV, NSA_GROUP, NSA_HD, D_MODEL).transpose(1, 0, 2, 3).reshape(qd, D_MODEL)
    aux_q = jnp.zeros((8, qd), F32).at[0].set(jnp.tile(nsa_q_norm[j], NSA_HEADS))
    aux_kv = (jnp.zeros((8, KV_DIM), F32).at[0].set(jnp.tile(nsa_k_norm[j, 0], NSA_KV))
              .at[1].set(jnp.tile(nsa_k_norm[j, 1], NSA_KV)))
    cw_k = _compress_weights(nsa_cmp_pos[j, 0], nsa_cmp_w1[j, 0], nsa_cmp_w2[j, 0], nsa_k_norm[j, 2])
    cw_v = _compress_weights(nsa_cmp_pos[j, 1], nsa_cmp_w1[j, 1], nsa_cmp_w2[j, 1], nsa_k_norm[j, 2])
    return (norm_w_i[None, :], _wsplit(w_q), _wsplit(w_kv), _wsplit(w_z), _wsplit(w_g), aux_q, aux_kv,
            cw_k, cw_v, _wsplit(w_o))


def _nsa_project_all(x2d, nw, tile):
    nw_row, w_q, w_kv, w_z, w_g, aux_q, aux_kv = nw[:7]
    qp = _nsa_proj("q", x2d, nw_row, w_q, aux_q, tile)
    kvs = _nsa_proj("kv", x2d, nw_row, w_kv, aux_kv, tile)
    sz = _nsa_proj("z", x2d, nw_row, w_z, aux_q, tile)
    gates = _nsa_proj("g", x2d, nw_row, w_g, aux_q, tile)
    return qp, kvs, sz, gates


def _nsa_prompt_layer(x2d, nw, toep, cbias, batch, seq, tile):
    qp, (kc, vc, ks, vs, kw, vw), sz, gates = _nsa_project_all(x2d, nw, tile)
    cw_k, cw_v, w_o = nw[7:]
    mc = seq // CMP_STRIDE
    b3 = lambda a: a.reshape(batch, seq, KV_DIM)
    src_spec = lambda h: [pl.BlockSpec((None, seq, LANE), lambda b: (b, 0, h))]
    kcmp = _compress_call([b3(kc)], src_spec, (batch,), 0, (), cw_k, True, mc, batch, lambda b: (b, 0, 0))
    vcmp = _compress_call([b3(vc)], src_spec, (batch,), 0, (), cw_v, False, mc, batch, lambda b: (b, 0, 0))
    o_parts = _nsa_prompt_attn(qp, gates, kcmp, vcmp, b3(ks), b3(vs), b3(kw), b3(vw), toep, cbias, batch, seq)
    x_new = _nsa_out(o_parts, sz, x2d, w_o, tile)
    return x_new, (kc, vc, ks, vs, kw, vw)


PROMPT_TILE = 512
PROMPT_CHUNK = 64
SAMPLE_CHUNK = 8


def kernel(x_prompt, x_sample, cache_cmp_k, cache_cmp_v, cache_sel_k, cache_sel_v, state_win_k, state_win_v, state_wkv, state_shift, page_table, norm_w, rel_bias, rw_mu, rw_w_rkvz, rw_w0, rw_w1, rw_w2, rw_a0, rw_a1, rw_a2, rw_v0, rw_v1, rw_v2, rw_g1, rw_g2, rw_k_k, rw_k_a, rw_r_k, rw_ln_w, rw_ln_b, rw_w_o, nsa_w_in, nsa_q_norm, nsa_k_norm, nsa_cmp_pos, nsa_cmp_w1, nsa_cmp_w2, nsa_w_o):
    bp, tp, d = x_prompt.shape
    bs, ts, _ = x_sample.shape
    depth = norm_w.shape[0]
    n_pages = page_table.shape[1]
    past = n_pages * PAGE
    win_buf = state_win_k.shape[2]
    tile = min(PROMPT_TILE, tp)
    ns = bs * ts

    xp = x_prompt.reshape(bp * tp, d)
    xs = x_sample.reshape(ns, d)
    flat = lambda c: c.reshape(c.shape[0], c.shape[1], c.shape[2], KV_DIM)
    caches = tuple(flat(c) for c in (cache_cmp_k, cache_cmp_v, cache_sel_k, cache_sel_v))
    wins = (flat(state_win_k), flat(state_win_v))

    toep = _bias_table(rel_bias, 9, LANE, LANE, LANE, 0, 1)
    cbias = _bias_table(rel_bias, tp // LANE, LANE, tp // CMP_STRIDE, LANE, -(CMP_BLOCK - 1), CMP_STRIDE)
    sbias = _bias_table(rel_bias, 1, T_PAD, past + LANE, 0, past, 1)[0]
    scbias = _bias_table(rel_bias, 1, T_PAD, past // CMP_STRIDE, 0, past - (CMP_BLOCK - 1), CMP_STRIDE)[0]

    to_tm = lambda a: a.reshape(bs, ts, d).transpose(1, 0, 2).reshape(ns, d)
    from_tm = lambda a: a.reshape(ts, bs, d).transpose(1, 0, 2).reshape(ns, d)
    p_to_seq = lambda a: a.reshape(bp, tp, d)
    p_from_seq = lambda a: a.reshape(bp * tp, d)
    s_to_seq = lambda a: _pad_to(a.reshape(ts, bs, d).transpose(1, 0, 2), 1, SAMPLE_CHUNK)
    s_from_seq = lambda a: a[:, :ts].transpose(1, 0, 2).reshape(ns, d)

    vf_p = vf_s = None
    p_wkv, p_shift, s_wkv, s_shift = [], [], [], []
    p_kv = [[] for _ in range(6)]
    s_kv = [[] for _ in range(6)]
    for i in range(depth):
        j = i // 2
        if i % 2 == 0:
            lw = _rwkv_weights(j, norm_w[i], rw_mu, rw_w_rkvz, rw_w0, rw_w1, rw_w2, rw_a0, rw_a1, rw_a2, rw_v0,
                               rw_v1, rw_v2, rw_g1, rw_g2, rw_k_k, rw_k_a, rw_r_k, rw_ln_w, rw_ln_b, rw_w_o)
            xp, sh, sblk, vf_p = _rwkv_layer(xp, jnp.zeros((bp, 1, d), F32), jnp.zeros((bp, RW_PAIRS, LANE, LANE), F32),
                                             vf_p, lw, tile, p_to_seq, p_from_seq, PROMPT_CHUNK)
            p_wkv.append(_unblock_state(sblk))
            p_shift.append(sh[:, 0])
            xs_tm, sh, sblk, vf_s = _rwkv_layer(to_tm(xs), state_shift[j][None], _blockdiag_state(state_wkv[j]),
                                                vf_s, lw, ns, s_to_seq, s_from_seq, SAMPLE_CHUNK)
            xs = from_tm(xs_tm)
            s_wkv.append(_unblock_state(sblk))
            s_shift.append(sh[0])
        else:
            nw = _nsa_weights(j, norm_w[i], nsa_w_in, nsa_q_norm, nsa_k_norm, nsa_cmp_pos, nsa_cmp_w1, nsa_cmp_w2,
                              nsa_w_o)
            xp, newp = _nsa_prompt_layer(xp, nw, toep, cbias, bp, tp, tile)
            xs, news = _nsa_sample_layer(j, xs, nw, caches, wins, page_table, sbias, scbias, ts, past)
            wb = min(WINDOW, tp)
            for m_ in range(4):
                p_kv[m_].append(newp[m_].reshape(bp, tp, NSA_KV, NSA_HD))
                s_kv[m_].append(news[m_].reshape(bs, ts, NSA_KV, NSA_HD))
            for m_ in (4, 5):
                p_kv[m_].append(newp[m_].reshape(bp, tp, NSA_KV, NSA_HD)[:, tp - wb:])
                s_kv[m_].append(news[m_].reshape(bs, win_buf, NSA_KV, NSA_HD))
    st = jnp.stack
    return (xp.reshape(bp, tp, d), xs.reshape(bs, ts, d), st(p_wkv), st(p_shift),
            st(p_kv[0]), st(p_kv[1]), st(p_kv[2]), st(p_kv[3]), st(p_kv[4]), st(p_kv[5]),
            st(s_wkv), st(s_shift),
            st(s_kv[0]), st(s_kv[1]), st(s_kv[2]), st(s_kv[3]), st(s_kv[4]), st(s_kv[5]))
```

```python
import functools
import math

import numpy as np
import jax
import jax.numpy as jnp
from jax import lax
from jax.experimental import pallas as pl
from jax.experimental.pallas import tpu as pltpu

F32 = jnp.float32
BF16 = jnp.bfloat16

D_MODEL = 1024
RW_HEAD = 64
RW_HEADS = 16
RW_PAIRS = 8
GN_EPS = 64e-5
RMS_EPS = 1e-6

NSA_HEADS = 16
NSA_KV = 4
NSA_GROUP = 4
NSA_HD = 64
KV_DIM = 256
CMP_BLOCK = 32
CMP_STRIDE = 16
SEL_BLOCK = 64
SEL_TOPN = 16
WINDOW = 512
NUM_BUCKETS = 32
MAX_DISTANCE = 1024
PAGE = 128

LANE = 128
SEL_PATH_PASSES = 1
VMEM_LIMIT = 48 * 1024 * 1024

NN = ((1,), (0,))
NT = ((1,), (1,))
TN = ((0,), (0,))
NEG_INF = float("-inf")


def _cp(*sem):
    return pltpu.CompilerParams(dimension_semantics=sem, vmem_limit_bytes=VMEM_LIMIT)


def _dg(a, b, dims=NN):
    return lax.dot_general(a, b, (dims, ((), ())), preferred_element_type=F32)


def _split2(x):
    hi = x.astype(BF16)
    lo = (x - hi.astype(F32)).astype(BF16)
    return hi, lo


def _split3(x):
    x0 = x.astype(BF16)
    r1 = x - x0.astype(F32)
    x1 = r1.astype(BF16)
    x2 = (r1 - x1.astype(F32)).astype(BF16)
    return x0, x1, x2


def _dot3s(a, b, dims=NN):
    return _dg(a[0], b[0], dims) + (_dg(a[0], b[1], dims) + _dg(a[1], b[0], dims))


def _dot3(a, b, dims=NN):
    return _dot3s(_split2(a), _split2(b), dims)


def _wdot(x, wh_ref, wl_ref, passes):
    if passes == 1:
        return _dg(x.astype(BF16), wh_ref[...])
    return _dot3s(_split2(x), (wh_ref[...], wl_ref[...]))


def _dot3_stacked(x, wh, wl):
    rows = x.shape[0]
    hi, lo = _split2(x)
    both = _dg(jnp.concatenate([hi, lo], axis=0), wh)
    return both[:rows] + (both[rows:] + _dg(hi, wl))


def _dotx(a, e, dims=NN):
    a0, a1, a2 = _split3(a)
    return _dg(a0, e, dims) + (_dg(a1, e, dims) + _dg(a2, e, dims))


def _xdot(e, b, dims=NN):
    b0, b1, b2 = _split3(b)
    return _dg(e, b0, dims) + (_dg(e, b1, dims) + _dg(e, b2, dims))


def _sigmoid(x):
    return 1.0 / (1.0 + jnp.exp(-x))


def _softplus(x):
    return jnp.maximum(x, 0.0) + jnp.log(1.0 + jnp.exp(-jnp.abs(x)))


def _rms_rows(x, w_row):
    ms = jnp.mean(x * x, axis=-1, keepdims=True)
    return x * lax.rsqrt(ms + RMS_EPS) * w_row


def _seg_mats(width):
    r = lax.broadcasted_iota(jnp.int32, (width, LANE), 0)
    c = lax.broadcasted_iota(jnp.int32, (width, LANE), 1)
    g = jnp.where(lax.shift_right_logical(r, 6) == c, 1.0, 0.0).astype(BF16)
    r2 = lax.broadcasted_iota(jnp.int32, (LANE, width), 0)
    c2 = lax.broadcasted_iota(jnp.int32, (LANE, width), 1)
    e = jnp.where(lax.shift_right_logical(c2, 6) == r2, 1.0, 0.0).astype(BF16)
    return g, e


def _segsum(x, g, e):
    return _dotx(_dotx(x, g), e)


def _masked_softmax_rows(logits, valid):
    lf = jnp.where(valid, logits, NEG_INF)
    m = jnp.max(lf, axis=-1, keepdims=True)
    m = jnp.where(m == NEG_INF, 0.0, m)
    ex = jnp.exp(lf - m)
    return ex / jnp.maximum(jnp.sum(ex, axis=-1, keepdims=True), 1e-30)


V_NORM, V_MU0, V_W0, V_A0, V_V0, V_KK, V_KA = 0, 1, 7, 8, 9, 10, 11
MU_R, MU_W, MU_K, MU_V, MU_A, MU_G = 0, 1, 2, 3, 4, 5


def _rwkv_proj_kernel(mode, shift, tiles_per_group, has_vres, *refs):
    if mode == "r":
        (x_ref, init_ref, vec_ref, wh_ref, wl_ref, l1h, l1l, l2h, l2l,
         r_ref, lw_ref, sh_ref, carry) = refs
    elif mode == "k":
        (x_ref, init_ref, vec_ref, wh_ref, wl_ref, l1h, l1l, l2h, l2l,
         k_ref, na_ref, b_ref, carry) = refs
    elif mode == "v":
        if has_vres:
            (x_ref, init_ref, vec_ref, wh_ref, wl_ref, l1h, l1l, l2h, l2l, vf_ref,
             v_ref, carry) = refs
        else:
            (x_ref, init_ref, vec_ref, wh_ref, wl_ref, v_ref, carry) = refs
    else:
        (x_ref, init_ref, vec_ref, wh_ref, wl_ref, l1h, l1l, l2h, l2l,
         gz_ref, carry) = refs

    tt = x_ref.shape[0]
    @pl.when((pl.program_id(0) % tiles_per_group) == 0)
    def _():
        carry[...] = init_ref[...]

    h = _rms_rows(x_ref[...], vec_ref[V_NORM:V_NORM + 1, :])
    prev_rows = carry[...]
    if shift == 1:
        rolled = pltpu.roll(h, 1, axis=0)
        rowi = lax.broadcasted_iota(jnp.int32, (tt, 1), 0)
        prev = jnp.where(rowi == 0, prev_rows, rolled)
    else:
        prev = jnp.concatenate([prev_rows, h[:tt - shift, :]], axis=0)
    carry[...] = h[tt - shift:, :]
    xx = prev - h

    def mix(i):
        return h + xx * vec_ref[V_MU0 + i:V_MU0 + i + 1, :]

    def big(xm):
        return _wdot(xm, wh_ref, wl_ref, 1)

    def lora(xm, act):
        return _wdot(act(_wdot(xm, l1h, l1l, 1)), l2h, l2l, 1)

    if mode == "r":
        r_ref[...] = big(mix(MU_R))
        wraw = vec_ref[V_W0:V_W0 + 1, :] + lora(mix(MU_W), jnp.tanh)
        w = -_softplus(-wraw) - 0.5
        lw_ref[...] = -jnp.exp(w)
        sh_ref[...] = h[tt - shift:, :]
    elif mode == "k":
        kraw = big(mix(MU_K))
        a = _sigmoid(vec_ref[V_A0:V_A0 + 1, :] + lora(mix(MU_A), lambda t: t))
        g, e = _seg_mats(D_MODEL)
        kk = kraw * vec_ref[V_KK:V_KK + 1, :]
        ss = _segsum(kk * kk, g, e)
        kk = kk / jnp.maximum(jnp.sqrt(ss), 1e-12)
        k_ref[...] = kraw * (1.0 + (a - 1.0) * vec_ref[V_KA:V_KA + 1, :])
        na_ref[...] = -kk
        b_ref[...] = kk * a
    elif mode == "v":
        xm = mix(MU_V)
        v = big(xm)
        if has_vres:
            vg = _sigmoid(vec_ref[V_V0:V_V0 + 1, :] + lora(xm, lambda t: t))
            v = v + (vf_ref[...] - v) * vg
        v_ref[...] = v
    else:
        xm = mix(MU_G)
        z = big(xm)
        gate = lora(xm, _sigmoid)
        gz_ref[...] = gate * (z * _sigmoid(z))


def _rwkv_proj(mode, x2d, init_prev, vecs, w_pair, lora_w, vfirst, tile):
    n = x2d.shape[0]
    groups, shift, _ = init_prev.shape
    tiles_per_group = (n // groups) // tile
    has_vres = mode == "v" and lora_w is not None
    row_spec = pl.BlockSpec((tile, D_MODEL), lambda i: (i, 0))
    init_spec = pl.BlockSpec((None, shift, D_MODEL), lambda i: (i // tiles_per_group, 0, 0))
    full = lambda a: pl.BlockSpec(a.shape, lambda i: (0,) * a.ndim)
    ins = [x2d, init_prev, vecs, w_pair[0], w_pair[1]]
    in_specs = [row_spec, init_spec, full(vecs), full(w_pair[0]), full(w_pair[1])]
    if lora_w is not None:
        ins += list(lora_w)
        in_specs += [full(a) for a in lora_w]
    if has_vres:
        ins.append(vfirst)
        in_specs.append(row_spec)
    act = jax.ShapeDtypeStruct((n, D_MODEL), F32)
    if mode == "r":
        out_shape = (act, act, jax.ShapeDtypeStruct(init_prev.shape, F32))
        out_specs = (row_spec, row_spec, init_spec)
    elif mode == "k":
        out_shape = (act, act, act)
        out_specs = (row_spec, row_spec, row_spec)
    else:
        out_shape = act
        out_specs = row_spec
    return pl.pallas_call(
        functools.partial(_rwkv_proj_kernel, mode, shift, tiles_per_group, has_vres),
        grid=(n // tile,),
        in_specs=in_specs,
        out_specs=out_specs,
        out_shape=out_shape,
        scratch_shapes=[pltpu.VMEM((shift, D_MODEL), F32)],
        compiler_params=_cp("arbitrary"),
        name="rwkv_proj_" + mode,
    )(*ins)


def _wkv_pairs(chunk, n_dbl, r, lw, kvec, v, na, bvec, s_old):
    ri = lax.broadcasted_iota(jnp.int32, (chunk, chunk), 0)
    ci = lax.broadcasted_iota(jnp.int32, (chunk, chunk), 1)
    tri = jnp.where(ci <= ri, 1.0, 0.0).astype(BF16)
    idx = range(len(r))
    each = lambda fn: [fn(i) for i in idx]
    cum = each(lambda i: _xdot(tri, lw[i]))
    total = each(lambda i: cum[i][chunk - 1:chunk, :])
    w_in = each(lambda i: jnp.exp(cum[i]))
    w_ex = each(lambda i: jnp.exp(cum[i] - lw[i]))
    w_inv = each(lambda i: jnp.exp(-cum[i]))
    w_rest = each(lambda i: jnp.exp(total[i] - cum[i]))

    lane = lax.broadcasted_iota(jnp.int32, (chunk, LANE), 1)
    m0 = lane < RW_HEAD

    def stack(x):
        return jnp.concatenate([jnp.where(m0, x, 0.0), jnp.where(m0, 0.0, x)], axis=0)

    b16 = lambda t: t.astype(BF16)
    a_s = each(lambda i: b16(stack(na[i] * w_ex[i])))
    r_s = each(lambda i: b16(stack(r[i] * w_in[i])))
    b_s = each(lambda i: b16(stack(bvec[i] * w_inv[i])))
    k_s = each(lambda i: b16(stack(kvec[i] * w_inv[i])))
    v_s = each(lambda i: b16(stack(v[i])))
    bh_s = each(lambda i: b16(stack(bvec[i] * w_rest[i])))
    kh_s = each(lambda i: b16(stack(kvec[i] * w_rest[i])))
    s_sp = each(lambda i: b16(s_old[i]))

    c2 = 2 * chunk
    r2 = lax.broadcasted_iota(jnp.int32, (c2, c2), 0)
    q2 = lax.broadcasted_iota(jnp.int32, (c2, c2), 1)
    strict = q2 < r2
    incl = q2 <= r2
    l_ab = each(lambda i: jnp.where(strict, _dg(a_s[i], b_s[i], NT), 0.0))
    l_ak = each(lambda i: jnp.where(strict, _dg(a_s[i], k_s[i], NT), 0.0))
    m_rb = each(lambda i: jnp.where(incl, _dg(r_s[i], b_s[i], NT), 0.0))
    m_rk = each(lambda i: jnp.where(incl, _dg(r_s[i], k_s[i], NT), 0.0))

    x = each(lambda i: _dg(a_s[i], s_sp[i], NT) + _dg(b16(l_ak[i]), v_s[i]))
    p = l_ab
    for it in range(n_dbl):
        p_sp = each(lambda i: b16(p[i]))
        x = each(lambda i: x[i] + _dg(p_sp[i], b16(x[i])))
        if it < n_dbl - 1:
            p = each(lambda i: _dg(p_sp[i], p_sp[i]))
    u_s = each(lambda i: b16(x[i]))
    y = each(lambda i: _dg(r_s[i], s_sp[i], NT) + _dg(b16(m_rb[i]), u_s[i]) + _dg(b16(m_rk[i]), v_s[i]))
    s_new = each(lambda i: s_old[i] * jnp.exp(total[i])
                 + (_dg(u_s[i], bh_s[i], TN) + _dg(v_s[i], kh_s[i], TN)))
    return each(lambda i: y[i][:chunk, :] + y[i][chunk:, :]), s_new


WKV_PAIRS_PER_STEP = 8


def _wkv_kernel(chunk, n_dbl, r_ref, lw_ref, k_ref, v_ref, na_ref, b_ref, s0_ref,
                y_ref, sout_ref, s_scr):
    c = pl.program_id(2)

    @pl.when(c == 0)
    def _():
        s_scr[...] = s0_ref[...]

    pairs = range(WKV_PAIRS_PER_STEP)
    cols = lambda ref: [ref[:, pp * LANE:(pp + 1) * LANE] for pp in pairs]
    y, s_new = _wkv_pairs(chunk, n_dbl, cols(r_ref), cols(lw_ref), cols(k_ref), cols(v_ref), cols(na_ref),
                          cols(b_ref), [s_scr[pp] for pp in pairs])
    for pp in pairs:
        y_ref[:, pp * LANE:(pp + 1) * LANE] = y[pp]
        s_scr[pp] = s_new[pp]

    @pl.when(c == pl.num_programs(2) - 1)
    def _():
        sout_ref[...] = s_scr[...]


def _wkv(r, lw, k, v, na, b, s0blk, chunk):
    g, t, _ = r.shape
    n_dbl = int(round(math.log2(chunk)))
    pps = WKV_PAIRS_PER_STEP
    seq_spec = pl.BlockSpec((None, chunk, pps * LANE), lambda gi, p, c: (gi, c, p))
    st_spec = pl.BlockSpec((None, pps, LANE, LANE), lambda gi, p, c: (gi, p, 0, 0))
    return pl.pallas_call(
        functools.partial(_wkv_kernel, chunk, n_dbl),
        grid=(g, RW_PAIRS // pps, t // chunk),
        in_specs=[seq_spec] * 6 + [st_spec],
        out_specs=(seq_spec, st_spec),
        out_shape=(jax.ShapeDtypeStruct(r.shape, F32), jax.ShapeDtypeStruct(s0blk.shape, F32)),
        scratch_shapes=[pltpu.VMEM((pps, LANE, LANE), F32)],
        compiler_params=_cp("arbitrary", "arbitrary", "arbitrary"),
        name="wkv_scan",
    )(r, lw, k, v, na, b, s0blk)


def _rwkv_out_kernel(y_ref, r_ref, k_ref, v_ref, gz_ref, x_ref, vec_ref, wh_ref, wl_ref, o_ref):
    g, e = _seg_mats(D_MODEL)
    y = y_ref[...]
    inv = 1.0 / RW_HEAD
    mean = _segsum(y, g, e) * inv
    d = y - mean
    var = _segsum(d * d, g, e) * inv
    yn = d * lax.rsqrt(var + GN_EPS) * vec_ref[1:2, :] + vec_ref[2:3, :]
    v = v_ref[...]
    bonus = _segsum(r_ref[...] * k_ref[...] * vec_ref[0:1, :], g, e) * v
    o = (yn + bonus) * gz_ref[...]
    o_ref[...] = x_ref[...] + _wdot(o, wh_ref, wl_ref, 1)


def _rwkv_out(y, r, k, v, gz, x2d, vecs, w_pair, tile):
    n = x2d.shape[0]
    row_spec = pl.BlockSpec((tile, D_MODEL), lambda i: (i, 0))
    full = lambda a: pl.BlockSpec(a.shape, lambda i: (0,) * a.ndim)
    return pl.pallas_call(
        _rwkv_out_kernel,
        grid=(n // tile,),
        in_specs=[row_spec] * 6 + [full(vecs), full(w_pair[0]), full(w_pair[1])],
        out_specs=row_spec,
        out_shape=jax.ShapeDtypeStruct((n, D_MODEL), F32),
        compiler_params=_cp("arbitrary"),
        name="rwkv_out",
    )(y, r, k, v, gz, x2d, vecs, w_pair[0], w_pair[1])


def _wsplit(w):
    hi = w.astype(BF16)
    lo = (w - hi.astype(F32)).astype(BF16)
    return hi, lo


def _pad_to(a, axis, size):
    pad = [(0, 0)] * a.ndim
    pad[axis] = (0, size - a.shape[axis])
    return jnp.pad(a, pad)


def _lora_pair(w1, w2):
    rank = w1.shape[1]
    rp = ((rank + LANE - 1) // LANE) * LANE
    return _wsplit(_pad_to(w1, 1, rp)) + _wsplit(_pad_to(w2, 0, rp))


def _blockdiag_state(s):
    g = s.shape[0]
    s = s.reshape(g, RW_PAIRS, 2, RW_HEAD, RW_HEAD)
    z = jnp.zeros_like(s[:, :, 0])
    top = jnp.concatenate([s[:, :, 0], z], axis=-1)
    bot = jnp.concatenate([z, s[:, :, 1]], axis=-1)
    return jnp.concatenate([top, bot], axis=-2)


def _unblock_state(sb):
    g = sb.shape[0]
    a = sb[:, :, :RW_HEAD, :RW_HEAD]
    b = sb[:, :, RW_HEAD:, RW_HEAD:]
    return jnp.stack([a, b], axis=2).reshape(g, RW_HEADS, RW_HEAD, RW_HEAD)


def _rwkv_layer(x2d, init_prev, s0blk, vfirst, lw, tile, to_seq, from_seq, chunk):
    vecs, w_r, w_k, w_v, w_z, lo_w, lo_a, lo_v, lo_g, vecs_out, w_o = lw
    r, logw, shift_out = _rwkv_proj("r", x2d, init_prev, vecs, w_r, lo_w, None, tile)
    k, na, b = _rwkv_proj("k", x2d, init_prev, vecs, w_k, lo_a, None, tile)
    v = _rwkv_proj("v", x2d, init_prev, vecs, w_v, lo_v, vfirst, tile)
    gz = _rwkv_proj("z", x2d, init_prev, vecs, w_z, lo_g, None, tile)
    if vfirst is None:
        vfirst = v
    y, sblk = _wkv(*(to_seq(t) for t in (r, logw, k, v, na, b)), s0blk, chunk)
    x_new = _rwkv_out(from_seq(y), r, k, v, gz, x2d, vecs_out, w_o, tile)
    return x_new, shift_out, sblk, vfirst


def _rwkv_weights(j, norm_w_i, rw_mu, rw_w_rkvz, rw_w0, rw_w1, rw_w2, rw_a0, rw_a1, rw_a2, rw_v0, rw_v1,
                  rw_v2, rw_g1, rw_g2, rw_k_k, rw_k_a, rw_r_k, rw_ln_w, rw_ln_b, rw_w_o):
    zero = jnp.zeros((D_MODEL,), F32)
    v0 = rw_v0[j - 1] if j > 0 else zero
    vecs = jnp.stack([norm_w_i] + [rw_mu[j, i] for i in range(6)]
                     + [rw_w0[j], rw_a0[j], v0, rw_k_k[j], rw_k_a[j]] + [zero] * 4)
    lo_v = _lora_pair(rw_v1[j - 1], rw_v2[j - 1]) if j > 0 else None
    vecs_out = jnp.stack([rw_r_k[j].reshape(D_MODEL), rw_ln_w[j], rw_ln_b[j]] + [zero] * 5)
    return (vecs, _wsplit(rw_w_rkvz[j, 0]), _wsplit(rw_w_rkvz[j, 1]), _wsplit(rw_w_rkvz[j, 2]),
            _wsplit(rw_w_rkvz[j, 3]), _lora_pair(rw_w1[j], rw_w2[j]), _lora_pair(rw_a1[j], rw_a2[j]),
            lo_v, _lora_pair(rw_g1[j], rw_g2[j]), vecs_out, _wsplit(rw_w_o[j]))


def _seg_rms(y, aux_row, width):
    g, e = _seg_mats(width)
    ms = _segsum(y * y, g, e) * (1.0 / NSA_HD)
    return y * lax.rsqrt(ms + RMS_EPS) * aux_row


def _nsa_proj_kernel(mode, x_ref, nw_ref, wh_ref, wl_ref, aux_ref, *out_refs):
    h = _rms_rows(x_ref[...], nw_ref[...])
    y = _wdot(h, wh_ref, wl_ref, SEL_PATH_PASSES if mode in ("q", "kv") else 1)
    if mode == "q":
        out_refs[0][...] = _seg_rms(y, aux_ref[0:1, :], D_MODEL) * (NSA_HD ** -0.5)
    elif mode == "kv":
        kc, vc, ks, vs, kw, vw = (y[:, i * KV_DIM:(i + 1) * KV_DIM] for i in range(6))
        out_refs[0][...] = kc
        out_refs[1][...] = vc
        out_refs[2][...] = _seg_rms(ks, aux_ref[0:1, :], KV_DIM)
        out_refs[3][...] = vs
        out_refs[4][...] = _seg_rms(kw, aux_ref[1:2, :], KV_DIM)
        out_refs[5][...] = vw
    elif mode == "z":
        out_refs[0][...] = y * _sigmoid(y)
    else:
        out_refs[0][...] = _sigmoid(y)


def _nsa_proj(mode, x2d, nw_row, w_pair, aux, tile):
    n = x2d.shape[0]
    width = w_pair[0].shape[1]
    row_spec = pl.BlockSpec((tile, D_MODEL), lambda i: (i, 0))
    full = lambda a: pl.BlockSpec(a.shape, lambda i: (0,) * a.ndim)
    if mode == "kv":
        out_shape = tuple(jax.ShapeDtypeStruct((n, KV_DIM), F32) for _ in range(6))
        out_specs = tuple(pl.BlockSpec((tile, KV_DIM), lambda i: (i, 0)) for _ in range(6))
    else:
        out_shape = (jax.ShapeDtypeStruct((n, width), F32),)
        out_specs = (pl.BlockSpec((tile, width), lambda i: (i, 0)),)
    res = pl.pallas_call(
        functools.partial(_nsa_proj_kernel, mode),
        grid=(n // tile,),
        in_specs=[row_spec, full(nw_row), full(w_pair[0]), full(w_pair[1]), full(aux)],
        out_specs=out_specs,
        out_shape=out_shape,
        compiler_params=_cp("arbitrary"),
        name="nsa_proj_" + mode,
    )(x2d, nw_row, w_pair[0], w_pair[1], aux)
    return res if mode == "kv" else res[0]


def _compress_kernel(n_prefetch, n_src, m, do_norm, *refs):
    refs = refs[n_prefetch:]
    src_lo = refs[:n_src]
    src_hi = refs[n_src:2 * n_src]
    pos_ref, wah, wal, wbh, wbl, w2h, w2l, nrm_ref, o_ref = refs[2 * n_src:]
    rows_per_src = m // n_src
    acc_a = jnp.zeros((m, KV_DIM), F32)
    acc_b = jnp.zeros((m, KV_DIM), F32)

    def strided(srcs, jj):
        parts = [s[pl.ds(jj, rows_per_src, stride=CMP_STRIDE), :] for s in srcs]
        return parts[0] if n_src == 1 else jnp.concatenate(parts, axis=0)

    for jj in range(CMP_STRIDE):
        x = jnp.concatenate([strided(src_lo, jj), strided(src_hi, jj)], axis=-1)
        xa = x + pos_ref[jj:jj + 1, :]
        xb = x + pos_ref[CMP_STRIDE + jj:CMP_STRIDE + jj + 1, :]
        if do_norm and SEL_PATH_PASSES == 3:
            acc_a = acc_a + _dot3_stacked(xa, wah[jj], wal[jj])
            acc_b = acc_b + _dot3_stacked(xb, wbh[jj], wbl[jj])
        else:
            acc_a = acc_a + _dg(xa.astype(BF16), wah[jj])
            acc_b = acc_b + _dg(xb.astype(BF16), wbh[jj])
    pre = acc_a + pltpu.roll(acc_b, m - 1, axis=0)
    hid = pre * _sigmoid(pre)
    out = _dot3s(_split2(hid), (w2h[...], w2l[...]))
    if do_norm:
        out = _seg_rms(out, nrm_ref[0:1, :], KV_DIM)
    o_ref[...] = out


def _compress_call(srcs, half_specs, grid, n_prefetch, prefetch, cw, do_norm, m, batch, idx):
    pos, wa, wb, w2, nrm = cw
    consts = [pos, wa[0], wa[1], wb[0], wb[1], w2[0], w2[1], nrm]
    cspecs = [pl.BlockSpec(a.shape, functools.partial(lambda nd, *_: (0,) * nd, a.ndim)) for a in consts]
    out_spec = pl.BlockSpec((None, m, KV_DIM), idx)
    gs = pltpu.PrefetchScalarGridSpec(
        num_scalar_prefetch=n_prefetch, grid=grid, in_specs=half_specs(0) + half_specs(1) + cspecs,
        out_specs=out_spec)
    return pl.pallas_call(
        functools.partial(_compress_kernel, n_prefetch, len(srcs), m, do_norm),
        grid_spec=gs,
        out_shape=jax.ShapeDtypeStruct((batch, m, KV_DIM), F32),
        compiler_params=_cp("arbitrary"),
        name="nsa_compress",
    )(*prefetch, *srcs, *srcs, *consts)


def _compress_weights(pos, w1, w2, knorm_row):
    eye = jnp.eye(NSA_KV, dtype=F32)
    w1r = w1.reshape(CMP_BLOCK, NSA_HD, NSA_HD)
    blk = jnp.einsum("ab,jde->jadbe", eye, w1r).reshape(CMP_BLOCK, KV_DIM, KV_DIM)
    w2b = jnp.einsum("ab,de->adbe", eye, w2).reshape(KV_DIM, KV_DIM)
    nrm = jnp.zeros((8, KV_DIM), F32).at[0].set(jnp.tile(knorm_row, NSA_KV))
    return (jnp.tile(pos, (1, NSA_KV)), _wsplit(blk[:CMP_STRIDE]), _wsplit(blk[CMP_STRIDE:]), _wsplit(w2b), nrm)


def _bucket_thresholds():
    n = np.arange(0, 8192, dtype=np.float64)
    max_exact = NUM_BUCKETS // 2
    large = max_exact + np.floor(
        np.log(np.maximum(n, 1.0) / max_exact) / math.log(MAX_DISTANCE / max_exact) * (NUM_BUCKETS - max_exact))
    bucket = np.where(n < max_exact, n, np.minimum(large, NUM_BUCKETS - 1)).astype(np.int64)
    return [int(np.argmax(bucket >= m)) for m in range(NUM_BUCKETS)]


def _bias_kernel(thr, base_mul, base_add, col_stride, tab_ref, o_ref):
    i = pl.program_id(0)
    h = pl.program_id(1)
    rows, cols = o_ref.shape
    r = lax.broadcasted_iota(jnp.int32, (rows, cols), 0)
    c = lax.broadcasted_iota(jnp.int32, (rows, cols), 1)
    d = (i * base_mul + base_add) + r - c * col_stride
    val = jnp.full((rows, cols), tab_ref[0, h], F32)
    for m in range(1, NUM_BUCKETS):
        val = jnp.where(d >= thr[m], tab_ref[m, h], val)
    o_ref[...] = val


def _bias_table(rel_bias, n_i, rows, cols, base_mul, base_add, col_stride):
    return pl.pallas_call(
        functools.partial(_bias_kernel, _bucket_thresholds(), base_mul, base_add, col_stride),
        grid=(n_i, NSA_HEADS),
        in_specs=[pl.BlockSpec(memory_space=pltpu.SMEM)],
        out_specs=pl.BlockSpec((None, None, rows, cols), lambda i, h: (i, h, 0, 0)),
        out_shape=jax.ShapeDtypeStruct((n_i, NSA_HEADS, rows, cols), F32),
        compiler_params=_cp("arbitrary", "arbitrary"),
        name="nsa_bias_table",
    )(rel_bias)


def _overlap_t(n_cmp, n_cmp_pad, n_sel, n_sel_pad):
    cs = np.arange(n_cmp_pad)[None, :] * CMP_STRIDE
    ss = np.arange(n_sel_pad)[:, None] * SEL_BLOCK
    ov = np.maximum(np.minimum(cs + CMP_BLOCK, ss + SEL_BLOCK) - np.maximum(cs, ss), 0) / CMP_BLOCK
    ov = ov * (np.arange(n_cmp_pad)[None, :] < n_cmp) * (np.arange(n_sel_pad)[:, None] < n_sel)
    return jnp.asarray(ov, F32)


def _online_update(carry, q_sp, k_tiles, v_tiles, add_tiles, qk_passes=1):
    m, l, acc = carry
    if qk_passes == 1:
        lfs = [_dg(q_sp[0], k.astype(BF16), NT) + a for k, a in zip(k_tiles, add_tiles)]
    else:
        lfs = [_dot3s(q_sp, _split2(k), NT) + a for k, a in zip(k_tiles, add_tiles)]
    emax = functools.reduce(jnp.maximum, lfs)
    m_new = jnp.maximum(m, jnp.max(emax, axis=-1, keepdims=True))
    m_safe = jnp.where(m_new == NEG_INF, 0.0, m_new)
    alpha = jnp.exp(m - m_safe)
    ps = [jnp.exp(lf - m_safe) for lf in lfs]
    l = alpha * l + jnp.sum(functools.reduce(jnp.add, ps), axis=-1, keepdims=True)
    pv = functools.reduce(jnp.add, [_dg(p.astype(BF16), v.astype(BF16)) for p, v in zip(ps, v_tiles)])
    return m_new, l, alpha * acc + pv


def _online_update_b16(carry, q_b, k_tiles, v_tiles, add_tiles, mk):
    m, acc = carry
    lfs = [_dg(q_b, k.astype(BF16), NT).astype(BF16) + a for k, a in zip(k_tiles, add_tiles)]
    emax = functools.reduce(jnp.maximum, lfs).astype(F32)
    m_new = jnp.maximum(m, jnp.max(emax, axis=-1, keepdims=True))
    m_safe = jnp.where(m_new == NEG_INF, 0.0, m_new)
    alpha = jnp.exp(m - m_safe)
    m_b = jnp.broadcast_to(m_safe, lfs[0].shape).astype(BF16)
    pv = functools.reduce(jnp.add, [_dg(jnp.exp(lf - m_b), jnp.where(mk, v, 1.0).astype(BF16))
                                    for lf, v in zip(lfs, v_tiles)])
    return m_new, alpha * acc + pv


def _attn_init_b16(rows):
    return jnp.full((rows, 1), NEG_INF, F32), jnp.zeros((rows, LANE), F32)


def _attn_finish_b16(carry, mk):
    _, acc = carry
    denom = jnp.max(jnp.where(mk, 0.0, acc), axis=-1, keepdims=True)
    return acc / jnp.maximum(denom, 1e-30)


def _attn_init(rows):
    return (jnp.full((rows, 1), NEG_INF, F32), jnp.zeros((rows, 1), F32), jnp.zeros((rows, LANE), F32))


def _attn_finish(carry):
    _, l, acc = carry
    return acc / jnp.maximum(l, 1e-30)


def _select_blocks(imp_t, qpos_lane, sc_scr):
    n_sel = imp_t.shape[0]
    sidx = lax.broadcasted_iota(jnp.int32, imp_t.shape, 0)
    cur = lax.shift_right_logical(qpos_lane, 6)
    future = sidx * SEL_BLOCK > qpos_lane
    forced = (sidx == 0) | (sidx == cur) | (sidx == cur - 1)
    score = jnp.where(future, NEG_INF, jnp.where(forced, float("inf"), imp_t))
    sc_scr[...] = score

    def body(s2, cnt):
        rowv = sc_scr[pl.ds(s2, 1), :]
        ahead = (rowv > score) | ((rowv == score) & (s2 < sidx))
        return cnt + jnp.where(ahead, 1.0, 0.0)

    cnt = lax.fori_loop(0, n_sel, body, jnp.zeros(imp_t.shape, F32), unroll=4)
    return jnp.where(cnt < SEL_TOPN, 1.0, 0.0).astype(BF16)


def _block_mask(sel_t, kt):
    n_sel = sel_t.shape[0]
    sr = lax.broadcasted_iota(jnp.int32, (n_sel, LANE), 0)
    kc = lax.broadcasted_iota(jnp.int32, (n_sel, LANE), 1)
    expand = jnp.where(sr == 2 * kt + lax.shift_right_logical(kc, 6), 1.0, 0.0).astype(BF16)
    return _dg(sel_t, expand, TN)


def _lane_mask(par, rows):
    lane = lax.broadcasted_iota(jnp.int32, (rows, LANE), 1)
    lo = par * NSA_HD
    return (lane >= lo) & (lane < lo + NSA_HD)


SEL_GROUP = 8


def _nsa_prompt_kernel(n_sel, mc, *refs):
    q_refs = refs[0:4]
    gate_ref, kc_ref, vc_ref, ks_ref, vs_ref = refs[4:9]
    kw_refs = refs[9:14]
    vw_refs = refs[14:19]
    toep_ref, cb_ref, ovt_ref = refs[19:22]
    o_refs = refs[22:26]
    sc_scr = refs[26]

    qb = pl.program_id(1)
    par = pl.program_id(2) % 2
    mk = _lane_mask(par, LANE)
    q = jnp.concatenate([jnp.where(mk, r[...], 0.0) for r in q_refs], axis=0)
    q_sp = _split2(q)
    rows = NSA_GROUP * LANE
    qpos = qb * LANE + (lax.broadcasted_iota(jnp.int32, (rows, 1), 0) & (LANE - 1))

    cend = lax.broadcasted_iota(jnp.int32, (1, mc), 1) * CMP_STRIDE + (CMP_BLOCK - 1)
    if SEL_PATH_PASSES == 1:
        qk_c = _dg(q_sp[0], kc_ref[...].astype(BF16), NT)
    else:
        qk_c = _dot3s(q_sp, _split2(kc_ref[...]), NT)
    lc = qk_c + jnp.concatenate([cb_ref[g] for g in range(NSA_GROUP)], axis=0)
    p_c = _masked_softmax_rows(lc, cend <= qpos)
    o_c = _dg(p_c.astype(BF16), vc_ref[...].astype(BF16))
    psum = (p_c[0:LANE] + p_c[LANE:2 * LANE]) + (p_c[2 * LANE:3 * LANE] + p_c[3 * LANE:4 * LANE])
    imp_t = _dot3(ovt_ref[...], psum, NT)
    qpos_lane = qb * LANE + lax.broadcasted_iota(jnp.int32, (n_sel, LANE), 1)
    sel_t = _select_blocks(imp_t, qpos_lane, sc_scr)

    qrow = qb * LANE + lax.broadcasted_iota(jnp.int32, (LANE, LANE), 0)
    kcol = lax.broadcasted_iota(jnp.int32, (LANE, LANE), 1)

    neg = jnp.full((LANE, LANE), NEG_INF, BF16)
    mk2 = _lane_mask(par, 2 * LANE)

    def add_tile(ok, ti):
        return jnp.concatenate([jnp.where(ok, toep_ref[ti, g], neg) for g in range(NSA_GROUP)], axis=0)

    def sel_add(kt):
        ok = (_block_mask(sel_t, kt) > 0.5) & (kt * LANE + kcol <= qrow)
        return add_tile(ok, jnp.clip(qb - kt, 0, 8))

    def sel_body(it, carry):
        k_tiles, v_tiles, adds = [], [], []
        for i in range(0, SEL_GROUP, 2):
            kt = it * SEL_GROUP + i
            start = pl.multiple_of(kt * LANE, 2 * LANE)
            adds.append(jnp.concatenate([sel_add(kt), sel_add(kt + 1)], axis=1))
            k_tiles.append(ks_ref[pl.ds(start, 2 * LANE), :])
            v_tiles.append(vs_ref[pl.ds(start, 2 * LANE), :])
        return _online_update_b16(carry, q_sp[0], k_tiles, v_tiles, adds, mk2)

    n_it = (qb + SEL_GROUP) // SEL_GROUP
    mk_rows = _lane_mask(par, rows)
    o_s = _attn_finish_b16(lax.fori_loop(0, n_it, sel_body, _attn_init_b16(rows)), mk_rows)

    def win_add(i):
        kpos = (qb - 4 + i) * LANE + kcol
        dist = qrow - kpos
        return add_tile((kpos >= 0) & (dist >= 0) & (dist < WINDOW), 4 - i)

    pair = lambda refs, i: jnp.concatenate([refs[i][...], refs[i + 1][...]], axis=0)
    carry = _online_update_b16(
        _attn_init_b16(rows), q_sp[0], [pair(kw_refs, 0), pair(kw_refs, 2)], [pair(vw_refs, 0), pair(vw_refs, 2)],
        [jnp.concatenate([win_add(0), win_add(1)], axis=1), jnp.concatenate([win_add(2), win_add(3)], axis=1)], mk2)
    o_w = _attn_finish_b16(_online_update_b16(carry, q_sp[0], [kw_refs[4][...]], [vw_refs[4][...]], [win_add(4)], mk),
                           mk_rows)

    for g in range(NSA_GROUP):
        sl = slice(g * LANE, (g + 1) * LANE)
        og = (gate_ref[:, 3 * g:3 * g + 1] * o_c[sl] + gate_ref[:, 3 * g + 1:3 * g + 2] * o_s[sl]
              + gate_ref[:, 3 * g + 2:3 * g + 3] * o_w[sl])
        og = jnp.where(mk, og, 0.0)
        o_ref = o_refs[g]

        @pl.when(par == 0)
        def _():
            o_ref[...] = og

        @pl.when(par == 1)
        def _():
            o_ref[...] = o_ref[...] + og


def _nsa_prompt_attn(qp, gates, kcmp, vcmp, ks, vs, kw, vw, toep, cbias, batch, seq):
    nqb = seq // LANE
    mc = seq // CMP_STRIDE
    n_sel = seq // SEL_BLOCK
    ovt = _overlap_t(mc - 1, mc, n_sel, n_sel)
    q_specs = [pl.BlockSpec((LANE, LANE), functools.partial(lambda g, b, qb, kh: (b * nqb + qb, 2 * g + kh // 2), g))
               for g in range(NSA_GROUP)]
    half = lambda rows: pl.BlockSpec((None, rows, LANE), lambda b, qb, kh: (b, 0, kh // 2))
    win_specs = [pl.BlockSpec((None, LANE, LANE),
                              functools.partial(lambda i, b, qb, kh: (b, jnp.maximum(qb - 4 + i, 0), kh // 2), i))
                 for i in range(5)]
    in_specs = (q_specs
                + [pl.BlockSpec((LANE, LANE), lambda b, qb, kh: (b * nqb + qb, kh))]
                + [half(mc), half(mc), half(seq), half(seq)]
                + win_specs + win_specs
                + [pl.BlockSpec((9, NSA_GROUP, LANE, LANE), lambda b, qb, kh: (0, kh, 0, 0)),
                   pl.BlockSpec((None, NSA_GROUP, LANE, mc), lambda b, qb, kh: (qb, kh, 0, 0)),
                   pl.BlockSpec(ovt.shape, lambda b, qb, kh: (0, 0))])
    out_spec = pl.BlockSpec((LANE, LANE), lambda b, qb, kh: (b * nqb + qb, kh // 2))
    n = batch * seq
    return pl.pallas_call(
        functools.partial(_nsa_prompt_kernel, n_sel, mc),
        grid=(batch, nqb, NSA_KV),
        in_specs=in_specs,
        out_specs=(out_spec,) * NSA_GROUP,
        out_shape=tuple(jax.ShapeDtypeStruct((n, KV_DIM), F32) for _ in range(NSA_GROUP)),
        scratch_shapes=[pltpu.VMEM((n_sel, LANE), F32)],
        compiler_params=_cp("arbitrary", "arbitrary", "arbitrary"),
        name="nsa_prompt_attn",
    )(qp, qp, qp, qp, gates, kcmp, vcmp, ks, vs, *([kw] * 5), *([vw] * 5), toep.astype(BF16), cbias, ovt)


T_PAD = 8
SAMPLE_GROUP = 8


def _nsa_sample_kernel(past, dec_seq, n_pages, win_buf, *refs):
    n = n_pages
    t_new = dec_seq
    q_refs = refs[1:5]
    gate_ref, kc_ref, vc_ref = refs[5:8]
    ks_pages = refs[8:8 + n]
    vs_pages = refs[8 + n:8 + 2 * n]
    ksn, vsn, kwn, vwn, wk_ref, wv_ref, sb_ref, cb_ref, ovt_ref = refs[8 + 2 * n:17 + 2 * n]
    o_refs = refs[17 + 2 * n:21 + 2 * n]
    wko_ref, wvo_ref, sc_scr = refs[21 + 2 * n:24 + 2 * n]

    par = pl.program_id(1) % 2
    rows = NSA_GROUP * T_PAD
    mk8 = _lane_mask(par, T_PAD)
    q = jnp.concatenate([jnp.where(mk8, r[...], 0.0) for r in q_refs], axis=0)
    q_sp = _split2(q)
    qpos = past + (lax.broadcasted_iota(jnp.int32, (rows, 1), 0) & (T_PAD - 1))
    qpos8 = past + lax.broadcasted_iota(jnp.int32, (T_PAD, LANE), 0)
    kcol = lax.broadcasted_iota(jnp.int32, (T_PAD, LANE), 1)
    zpad = jnp.zeros((LANE - T_PAD, LANE), F32)

    def per_group(fn):
        return jnp.concatenate([fn(g) for g in range(NSA_GROUP)], axis=0)

    mc = kc_ref.shape[0]
    cend = lax.broadcasted_iota(jnp.int32, (1, mc), 1) * CMP_STRIDE + (CMP_BLOCK - 1)
    lc = _dot3s(q_sp, _split2(kc_ref[...]), NT) + per_group(lambda g: cb_ref[g])
    p_c = _masked_softmax_rows(lc, cend <= qpos)
    o_c = _dot3(p_c, vc_ref[...])
    psum = (p_c[0:T_PAD] + p_c[T_PAD:2 * T_PAD]) + (p_c[2 * T_PAD:3 * T_PAD] + p_c[3 * T_PAD:4 * T_PAD])
    imp_t = _dot3(ovt_ref[...], jnp.concatenate([psum, zpad], axis=0), NT)
    qpos_lane = past + lax.broadcasted_iota(jnp.int32, imp_t.shape, 1)
    sel_t = _select_blocks(imp_t, qpos_lane, sc_scr)

    def tile_add(kt, base, extra_ok):
        kpos = base + kcol
        ok = (kpos <= qpos8) & extra_ok(kpos)
        if kt is not None:
            ok = ok & (_block_mask(sel_t, kt)[:T_PAD, :] > 0.5)
        madd = jnp.where(ok, 0.0, NEG_INF)
        return per_group(lambda g: sb_ref[g, :, base:base + LANE] + madd)

    always = lambda kpos: kpos >= 0
    k_new = jnp.concatenate([ksn[...], zpad], axis=0)
    v_new = jnp.concatenate([vsn[...], zpad], axis=0)
    carry = _attn_init(rows)
    for p0 in range(0, n, SAMPLE_GROUP):
        ps = range(p0, min(p0 + SAMPLE_GROUP, n))
        carry = _online_update(carry, q_sp, [ks_pages[p][...] for p in ps], [vs_pages[p][...] for p in ps],
                               [tile_add(p, p * PAGE, always) for p in ps], qk_passes=3)
    o_s = _attn_finish(_online_update(carry, q_sp, [k_new], [v_new], [tile_add(n, past, always)], qk_passes=3))

    in_win = lambda kpos: (qpos8 - kpos) < WINDOW
    w0 = past - win_buf
    n_wt = win_buf // LANE
    kw_new = jnp.concatenate([kwn[...], zpad], axis=0)
    vw_new = jnp.concatenate([vwn[...], zpad], axis=0)
    o_w = _attn_finish(_online_update(
        _attn_init(rows), q_sp,
        [wk_ref[i * LANE:(i + 1) * LANE, :] for i in range(n_wt)] + [kw_new],
        [wv_ref[i * LANE:(i + 1) * LANE, :] for i in range(n_wt)] + [vw_new],
        [tile_add(None, w0 + i * LANE, in_win) for i in range(n_wt)] + [tile_add(None, past, in_win)], qk_passes=3))

    for g in range(NSA_GROUP):
        sl = slice(g * T_PAD, (g + 1) * T_PAD)
        og = (gate_ref[:, 3 * g:3 * g + 1] * o_c[sl] + gate_ref[:, 3 * g + 1:3 * g + 2] * o_s[sl]
              + gate_ref[:, 3 * g + 2:3 * g + 3] * o_w[sl])
        og = jnp.where(mk8, og, 0.0)
        o_ref = o_refs[g]

        @pl.when(par == 0)
        def _():
            o_ref[...] = og

        @pl.when(par == 1)
        def _():
            o_ref[...] = o_ref[...] + og

    rowi = lax.broadcasted_iota(jnp.int32, (LANE, LANE), 0)
    for src, new, dst in ((wk_ref, kw_new, wko_ref), (wv_ref, vw_new, wvo_ref)):
        shifted = pltpu.roll(src[...], win_buf - t_new, axis=0)
        tail = pltpu.roll(new, LANE - t_new, axis=0)
        dst[0:win_buf - LANE, :] = shifted[0:win_buf - LANE, :]
        dst[win_buf - LANE:win_buf, :] = jnp.where(rowi >= LANE - t_new, tail, shifted[win_buf - LANE:win_buf, :])


def _nsa_sample_attn(j, page_table, qp8, gates8, kcmp, vcmp, cache_k, cache_v, new8, win_k, win_v, sbias, scbias,
                     past, dec_seq):
    batch, n_pages = page_table.shape
    win_buf = win_k.shape[2]
    mc = kcmp.shape[1]
    n_sel = -(-(past + dec_seq) // SEL_BLOCK)
    n_sel_pad = -(-n_sel // 8) * 8
    ovt = _overlap_t(mc - 1, mc, n_sel, n_sel_pad)
    q_specs = [pl.BlockSpec((None, T_PAD, LANE), functools.partial(lambda g, b, kh, pt: (b, 0, 2 * g + kh // 2), g))
               for g in range(NSA_GROUP)]
    half = lambda rows: pl.BlockSpec((None, rows, LANE), lambda b, kh, pt: (b, 0, kh // 2))
    page_specs = [pl.BlockSpec((None, None, PAGE, LANE),
                               functools.partial(lambda p, b, kh, pt: (j, pt[b, p], 0, kh // 2), p))
                  for p in range(n_pages)]
    win_spec = pl.BlockSpec((None, None, win_buf, LANE), lambda b, kh, pt: (j, b, 0, kh // 2))
    in_specs = (q_specs
                + [pl.BlockSpec((None, T_PAD, LANE), lambda b, kh, pt: (b, 0, kh)), half(mc), half(mc)]
                + page_specs + page_specs
                + [half(T_PAD)] * 4 + [win_spec, win_spec]
                + [pl.BlockSpec((NSA_GROUP, T_PAD, sbias.shape[2]), lambda b, kh, pt: (kh, 0, 0)),
                   pl.BlockSpec((NSA_GROUP, T_PAD, mc), lambda b, kh, pt: (kh, 0, 0)),
                   pl.BlockSpec(ovt.shape, lambda b, kh, pt: (0, 0))])
    out_specs = (half(T_PAD),) * NSA_GROUP + (half(win_buf), half(win_buf))
    out_shape = (tuple(jax.ShapeDtypeStruct((batch, T_PAD, KV_DIM), F32) for _ in range(NSA_GROUP))
                 + tuple(jax.ShapeDtypeStruct((batch, win_buf, KV_DIM), F32) for _ in range(2)))
    gs = pltpu.PrefetchScalarGridSpec(
        num_scalar_prefetch=1, grid=(batch, NSA_KV), in_specs=in_specs, out_specs=out_specs,
        scratch_shapes=[pltpu.VMEM((n_sel_pad, LANE), F32)])
    res = pl.pallas_call(
        functools.partial(_nsa_sample_kernel, past, dec_seq, n_pages, win_buf),
        grid_spec=gs,
        out_shape=out_shape,
        compiler_params=_cp("arbitrary", "arbitrary"),
        name="nsa_sample_attn",
    )(page_table, qp8, qp8, qp8, qp8, gates8, kcmp, vcmp, *([cache_k] * n_pages), *([cache_v] * n_pages),
      *new8, win_k, win_v, sbias, scbias, ovt)
    return res[:NSA_GROUP], res[NSA_GROUP], res[NSA_GROUP + 1]


def _round_robin(gens):
    active = list(gens)
    while active:
        still = []
        for gen in active:
            try:
                next(gen)
                still.append(gen)
            except StopIteration:
                pass
        active = still


def _online_update_steps(carry, q_sp, k_tiles, v_tiles, add_tiles):
    m, l, acc = carry
    lfs = [_dot3s(q_sp, _split2(k), NT) + a for k, a in zip(k_tiles, add_tiles)]
    yield
    emax = functools.reduce(jnp.maximum, lfs)
    m_new = jnp.maximum(m, jnp.max(emax, axis=-1, keepdims=True))
    m_safe = jnp.where(m_new == NEG_INF, 0.0, m_new)
    alpha = jnp.exp(m - m_safe)
    ps = [jnp.exp(lf - m_safe) for lf in lfs]
    l = alpha * l + jnp.sum(functools.reduce(jnp.add, ps), axis=-1, keepdims=True)
    yield
    pv = functools.reduce(jnp.add, [_dg(p.astype(BF16), v.astype(BF16)) for p, v in zip(ps, v_tiles)])
    yield
    return m_new, l, alpha * acc + pv


def _nsa_decode_kernel(past, dec_seq, n_pages, win_buf, n_sel_pad, *refs):
    n = n_pages
    q_ref, gate_ref, kc_ref, vc_ref = refs[1:5]
    ks_pages = refs[5:5 + n]
    vs_pages = refs[5 + n:5 + 2 * n]
    ksn, vsn, kwn, vwn, wk_ref, wv_ref, sb_ref, cb_ref, ovt_ref = refs[5 + 2 * n:14 + 2 * n]
    o_refs = refs[14 + 2 * n:18 + 2 * n]
    wko_ref, wvo_ref = refs[18 + 2 * n:20 + 2 * n]

    rows = NSA_GROUP * T_PAD
    qpos = past + (lax.broadcasted_iota(jnp.int32, (rows, 1), 0) & (T_PAD - 1))
    qpos8 = past + lax.broadcasted_iota(jnp.int32, (T_PAD, LANE), 0)
    kcol = lax.broadcasted_iota(jnp.int32, (T_PAD, LANE), 1)
    zpad = jnp.zeros((LANE - T_PAD, LANE), F32)
    mc = kc_ref.shape[0]
    cend = lax.broadcasted_iota(jnp.int32, (1, mc), 1) * CMP_STRIDE + (CMP_BLOCK - 1)
    qpos_lane = past + lax.broadcasted_iota(jnp.int32, (n_sel_pad, LANE), 1)
    sidx = lax.broadcasted_iota(jnp.int32, (n_sel_pad, LANE), 0)
    always = lambda kpos: kpos >= 0
    in_win = lambda kpos: (qpos8 - kpos) < WINDOW
    w0 = past - win_buf
    n_wt = win_buf // LANE
    halves = range(KV_DIM // LANE)
    lanes = lambda h: slice(h * LANE, (h + 1) * LANE)

    def per_group(fn):
        return jnp.concatenate([fn(g) for g in range(NSA_GROUP)], axis=0)

    def new_tile(ref, h):
        return jnp.concatenate([ref[:, lanes(h)], zpad], axis=0)

    kw_new = [new_tile(kwn, h) for h in halves]
    vw_new = [new_tile(vwn, h) for h in halves]
    k_new = [new_tile(ksn, h) for h in halves]
    v_new = [new_tile(vsn, h) for h in halves]
    out = {}

    def chain(kh):
        h, par = kh // 2, kh % 2
        mk8 = _lane_mask(par, T_PAD)
        q = per_group(lambda g: jnp.where(mk8, q_ref[:, g * KV_DIM + h * LANE:g * KV_DIM + (h + 1) * LANE], 0.0))
        q_sp = _split2(q)
        lc = _dot3s(q_sp, _split2(kc_ref[:, lanes(h)]), NT) + per_group(lambda g: cb_ref[kh * NSA_GROUP + g])
        yield
        p_c = _masked_softmax_rows(lc, cend <= qpos)
        yield
        o_c = _dot3(p_c, vc_ref[:, lanes(h)])
        psum = (p_c[0:T_PAD] + p_c[T_PAD:2 * T_PAD]) + (p_c[2 * T_PAD:3 * T_PAD] + p_c[3 * T_PAD:4 * T_PAD])
        imp_t = _dot3(ovt_ref[...], jnp.concatenate([psum, zpad], axis=0), NT)
        yield
        cur = lax.shift_right_logical(qpos_lane, 6)
        future = sidx * SEL_BLOCK > qpos_lane
        forced = (sidx == 0) | (sidx == cur) | (sidx == cur - 1)
        score = jnp.where(future, NEG_INF, jnp.where(forced, float("inf"), imp_t))
        cnt = jnp.zeros(score.shape, F32)
        for s2 in range(n_sel_pad):
            rowv = score[s2:s2 + 1, :]
            ahead = (rowv > score) | ((rowv == score) & (s2 < sidx))
            cnt = cnt + jnp.where(ahead, 1.0, 0.0)
            if s2 % 8 == 7:
                yield
        sel_t = jnp.where(cnt < SEL_TOPN, 1.0, 0.0).astype(BF16)

        def tile_add(kt, base, extra_ok):
            kpos = base + kcol
            ok = (kpos <= qpos8) & extra_ok(kpos)
            if kt is not None:
                ok = ok & (_block_mask(sel_t, kt)[:T_PAD, :] > 0.5)
            madd = jnp.where(ok, 0.0, NEG_INF)
            return per_group(lambda g: sb_ref[kh * NSA_GROUP + g, :, base:base + LANE] + madd)

        carry = _attn_init(rows)
        for p0 in range(0, n, SAMPLE_GROUP):
            ps = range(p0, min(p0 + SAMPLE_GROUP, n))
            carry = yield from _online_update_steps(
                carry, q_sp, [ks_pages[p][:, lanes(h)] for p in ps], [vs_pages[p][:, lanes(h)] for p in ps],
                [tile_add(p, p * PAGE, always) for p in ps])
        carry = yield from _online_update_steps(carry, q_sp, [k_new[h]], [v_new[h]], [tile_add(n, past, always)])
        o_s = _attn_finish(carry)
        carry = yield from _online_update_steps(
            _attn_init(rows), q_sp,
            [wk_ref[i * LANE:(i + 1) * LANE, lanes(h)] for i in range(n_wt)] + [kw_new[h]],
            [wv_ref[i * LANE:(i + 1) * LANE, lanes(h)] for i in range(n_wt)] + [vw_new[h]],
            [tile_add(None, w0 + i * LANE, in_win) for i in range(n_wt)] + [tile_add(None, past, in_win)])
        o_w = _attn_finish(carry)
        for g in range(NSA_GROUP):
            sl = slice(g * T_PAD, (g + 1) * T_PAD)
            c0 = kh * LANE + g * 3
            og = (gate_ref[:, c0:c0 + 1] * o_c[sl] + gate_ref[:, c0 + 1:c0 + 2] * o_s[sl]
                  + gate_ref[:, c0 + 2:c0 + 3] * o_w[sl])
            out[kh, g] = jnp.where(mk8, og, 0.0)

    _round_robin([chain(kh) for kh in range(NSA_KV)])
    for g in range(NSA_GROUP):
        o_refs[g][...] = jnp.concatenate([out[2 * h, g] + out[2 * h + 1, g] for h in halves], axis=-1)

    rowi = lax.broadcasted_iota(jnp.int32, (LANE, LANE), 0)
    for src, new, dst in ((wk_ref, kw_new, wko_ref), (wv_ref, vw_new, wvo_ref)):
        shifted = pltpu.roll(src[...], win_buf - dec_seq, axis=0)
        tail = jnp.concatenate([pltpu.roll(t, LANE - dec_seq, axis=0) for t in new], axis=-1)
        rowk = jnp.concatenate([rowi] * (KV_DIM // LANE), axis=-1)
        dst[0:win_buf - LANE, :] = shifted[0:win_buf - LANE, :]
        dst[win_buf - LANE:win_buf, :] = jnp.where(rowk >= LANE - dec_seq, tail, shifted[win_buf - LANE:win_buf, :])


def _nsa_decode_attn(j, page_table, qp8, gates8, kcmp, vcmp, cache_k, cache_v, new8, win_k, win_v, sbias, scbias,
                     past, dec_seq):
    batch, n_pages = page_table.shape
    win_buf = win_k.shape[2]
    mc = kcmp.shape[1]
    n_sel = -(-(past + dec_seq) // SEL_BLOCK)
    n_sel_pad = -(-n_sel // 8) * 8
    ovt = _overlap_t(mc - 1, mc, n_sel, n_sel_pad)
    per_b = lambda a: pl.BlockSpec((None,) + a.shape[1:], lambda b, pt: (b, 0, 0))
    const = lambda a: pl.BlockSpec(a.shape, functools.partial(lambda nd, b, pt: (0,) * nd, a.ndim))
    page_specs = [pl.BlockSpec((None, None, PAGE, KV_DIM), functools.partial(lambda p, b, pt: (j, pt[b, p], 0, 0), p))
                  for p in range(n_pages)]
    win_spec = pl.BlockSpec((None, None, win_buf, KV_DIM), lambda b, pt: (j, b, 0, 0))
    in_specs = ([per_b(qp8), per_b(gates8), per_b(kcmp), per_b(vcmp)] + page_specs + page_specs
                + [per_b(a) for a in new8] + [win_spec, win_spec, const(sbias), const(scbias), const(ovt)])
    o_sds = jax.ShapeDtypeStruct((batch, T_PAD, KV_DIM), F32)
    w_sds = jax.ShapeDtypeStruct((batch, win_buf, KV_DIM), F32)
    out_shape = (o_sds,) * NSA_GROUP + (w_sds, w_sds)
    gs = pltpu.PrefetchScalarGridSpec(
        num_scalar_prefetch=1, grid=(batch,), in_specs=in_specs, out_specs=tuple(per_b(s) for s in out_shape))
    res = pl.pallas_call(
        functools.partial(_nsa_decode_kernel, past, dec_seq, n_pages, win_buf, n_sel_pad),
        grid_spec=gs,
        out_shape=out_shape,
        compiler_params=_cp("arbitrary"),
        name="nsa_decode_attn",
    )(page_table, qp8, gates8, kcmp, vcmp, *([cache_k] * n_pages), *([cache_v] * n_pages),
      *new8, win_k, win_v, sbias, scbias, ovt)
    return res[:NSA_GROUP], res[NSA_GROUP], res[NSA_GROUP + 1]


def _nsa_sample_layer(j, x2d, nw, caches, wins, page_table, sbias, scbias, dec_seq, past):
    batch, n_pages = page_table.shape
    n = x2d.shape[0]
    qp, kvs, sz, gates = _nsa_project_all(x2d, nw, n)
    cw_k, cw_v, w_o = nw[7:]
    m = n_pages * PAGE // CMP_STRIDE
    page_specs = lambda h: [pl.BlockSpec((None, None, PAGE, LANE),
                                         functools.partial(lambda p, b, pt: (j, pt[b, p], 0, h), p))
                            for p in range(n_pages)]
    out_idx = lambda b, pt: (b, 0, 0)
    kcmp = _compress_call([caches[0]] * n_pages, page_specs, (batch,), 1, (page_table,), cw_k, True, m, batch, out_idx)
    vcmp = _compress_call([caches[1]] * n_pages, page_specs, (batch,), 1, (page_table,), cw_v, False, m, batch, out_idx)
    pad8 = lambda a: _pad_to(a.reshape(batch, dec_seq, a.shape[-1]), 1, T_PAD)
    new8 = tuple(pad8(kvs[i]) for i in (2, 3, 4, 5))
    o8, win_k_new, win_v_new = _nsa_decode_attn(j, page_table, pad8(qp), pad8(gates), kcmp, vcmp, caches[2], caches[3],
                                                new8, wins[0], wins[1], sbias, scbias, past, dec_seq)
    o_parts = [o[:, :dec_seq].reshape(n, KV_DIM) for o in o8]
    x_new = _nsa_out(o_parts, sz, x2d, w_o, n)
    return x_new, kvs[:4] + (win_k_new, win_v_new)


def _nsa_out_kernel(o0, o1, o2, o3, sz_ref, x_ref, wh_ref, wl_ref, out_ref):
    o = jnp.concatenate([o0[...], o1[...], o2[...], o3[...]], axis=-1) * sz_ref[...]
    out_ref[...] = x_ref[...] + _wdot(o, wh_ref, wl_ref, 1)


def _nsa_out(o_parts, sz, x2d, w_pair, tile):
    n = x2d.shape[0]
    row_spec = pl.BlockSpec((tile, D_MODEL), lambda i: (i, 0))
    part_spec = pl.BlockSpec((tile, KV_DIM), lambda i: (i, 0))
    full = lambda a: pl.BlockSpec(a.shape, lambda i: (0,) * a.ndim)
    return pl.pallas_call(
        _nsa_out_kernel,
        grid=(n // tile,),
        in_specs=[part_spec] * 4 + [row_spec, row_spec, full(w_pair[0]), full(w_pair[1])],
        out_specs=row_spec,
        out_shape=jax.ShapeDtypeStruct((n, D_MODEL), F32),
        compiler_params=_cp("arbitrary"),
        name="nsa_out",
    )(*o_parts, sz, x2d, w_pair[0], w_pair[1])


def _nsa_weights(j, norm_w_i, nsa_w_in, nsa_q_norm, nsa_k_norm, nsa_cmp_pos, nsa_cmp_w1, nsa_cmp_w2, nsa_w_o):
    w_in = nsa_w_in[j]
    qd = NSA_HEADS * NSA_HD

    def perm_cols(w):
        return w.reshape(D_MODEL, NSA_KV, NSA_GROUP, NSA_HD).transpose(0, 2, 1, 3).reshape(D_MODEL, qd)

    w_q = perm_cols(w_in[:, :qd])
    w_kv = w_in[:, qd:qd + 6 * KV_DIM]
    w_z = perm_cols(w_in[:, qd + 6 * KV_DIM:2 * qd + 6 * KV_DIM])
    w_g = w_in[:, 2 * qd + 6 * KV_DIM:].reshape(D_MODEL, NSA_KV, 3 * NSA_GROUP)
    w_g = _pad_to(w_g, 2, LANE).reshape(D_MODEL, NSA_KV * LANE)
    w_o = nsa_w_o[j].reshape(NSA_KV, NSA_GROUP, NSA_HD, D_MODEL).transpose(1, 0, 2, 3).reshape(qd, D_MODEL)
    aux_q = jnp.zeros((8, qd), F32).at[0].set(jnp.tile(nsa_q_norm[j], NSA_HEADS))
    aux_kv = (jnp.zeros((8, KV_DIM), F32).at[0].set(jnp.tile(nsa_k_norm[j, 0], NSA_KV))
              .at[1].set(jnp.tile(nsa_k_norm[j, 1], NSA_KV)))
    cw_k = _compress_weights(nsa_cmp_pos[j, 0], nsa_cmp_w1[j, 0], nsa_cmp_w2[j, 0], nsa_k_norm[j, 2])
    cw_v = _compress_weights(nsa_cmp_pos[j, 1], nsa_cmp_w1[j, 1], nsa_cmp_w2[j, 1], nsa_k_norm[j, 2])
    return (norm_w_i[None, :], _wsplit(w_q), _wsplit(w_kv), _wsplit(w_z), _wsplit(w_g), aux_q, aux_kv,
            cw_k, cw_v, _wsplit(w_o))


def _nsa_project_all(x2d, nw, tile):
    nw_row, w_q, w_kv, w_z, w_g, aux_q, aux_kv = nw[:7]
    qp = _nsa_proj("q", x2d, nw_row, w_q, aux_q, tile)
    kvs = _nsa_proj("kv", x2d, nw_row, w_kv, aux_kv, tile)
    sz = _nsa_proj("z", x2d, nw_row, w_z, aux_q, tile)
    gates = _nsa_proj("g", x2d, nw_row, w_g, aux_q, tile)
    return qp, kvs, sz, gates


def _nsa_prompt_layer(x2d, nw, toep, cbias, batch, seq, tile):
    qp, (kc, vc, ks, vs, kw, vw), sz, gates = _nsa_project_all(x2d, nw, tile)
    cw_k, cw_v, w_o = nw[7:]
    mc = seq // CMP_STRIDE
    b3 = lambda a: a.reshape(batch, seq, KV_DIM)
    src_spec = lambda h: [pl.BlockSpec((None, seq, LANE), lambda b: (b, 0, h))]
    kcmp = _compress_call([b3(kc)], src_spec, (batch,), 0, (), cw_k, True, mc, batch, lambda b: (b, 0, 0))
    vcmp = _compress_call([b3(vc)], src_spec, (batch,), 0, (), cw_v, False, mc, batch, lambda b: (b, 0, 0))
    o_parts = _nsa_prompt_attn(qp, gates, kcmp, vcmp, b3(ks), b3(vs), b3(kw), b3(vw), toep, cbias, batch, seq)
    x_new = _nsa_out(o_parts, sz, x2d, w_o, tile)
    return x_new, (kc, vc, ks, vs, kw, vw)


PROMPT_TILE = 512
PROMPT_CHUNK = 64
SAMPLE_CHUNK = 8


def kernel(x_prompt, x_sample, cache_cmp_k, cache_cmp_v, cache_sel_k, cache_sel_v, state_win_k, state_win_v, state_wkv, state_shift, page_table, norm_w, rel_bias, rw_mu, rw_w_rkvz, rw_w0, rw_w1, rw_w2, rw_a0, rw_a1, rw_a2, rw_v0, rw_v1, rw_v2, rw_g1, rw_g2, rw_k_k, rw_k_a, rw_r_k, rw_ln_w, rw_ln_b, rw_w_o, nsa_w_in, nsa_q_norm, nsa_k_norm, nsa_cmp_pos, nsa_cmp_w1, nsa_cmp_w2, nsa_w_o):
    bp, tp, d = x_prompt.shape
    bs, ts, _ = x_sample.shape
    depth = norm_w.shape[0]
    n_pages = page_table.shape[1]
    past = n_pages * PAGE
    win_buf = state_win_k.shape[2]
    tile = min(PROMPT_TILE, tp)
    ns = bs * ts

    xp = x_prompt.reshape(bp * tp, d)
    xs = x_sample.reshape(ns, d)
    flat = lambda c: c.reshape(c.shape[0], c.shape[1], c.shape[2], KV_DIM)
    caches = tuple(flat(c) for c in (cache_cmp_k, cache_cmp_v, cache_sel_k, cache_sel_v))
    wins = (flat(state_win_k), flat(state_win_v))

    toep = _bias_table(rel_bias, 9, LANE, LANE, LANE, 0, 1)
    cbias = _bias_table(rel_bias, tp // LANE, LANE, tp // CMP_STRIDE, LANE, -(CMP_BLOCK - 1), CMP_STRIDE)
    sbias = _bias_table(rel_bias, 1, T_PAD, past + LANE, 0, past, 1)[0]
    scbias = _bias_table(rel_bias, 1, T_PAD, past // CMP_STRIDE, 0, past - (CMP_BLOCK - 1), CMP_STRIDE)[0]

    to_tm = lambda a: a.reshape(bs, ts, d).transpose(1, 0, 2).reshape(ns, d)
    from_tm = lambda a: a.reshape(ts, bs, d).transpose(1, 0, 2).reshape(ns, d)
    p_to_seq = lambda a: a.reshape(bp, tp, d)
    p_from_seq = lambda a: a.reshape(bp * tp, d)
    s_to_seq = lambda a: _pad_to(a.reshape(ts, bs, d).transpose(1, 0, 2), 1, SAMPLE_CHUNK)
    s_from_seq = lambda a: a[:, :ts].transpose(1, 0, 2).reshape(ns, d)

    vf_p = vf_s = None
    p_wkv, p_shift, s_wkv, s_shift = [], [], [], []
    p_kv = [[] for _ in range(6)]
    s_kv = [[] for _ in range(6)]
    for i in range(depth):
        j = i // 2
        if i % 2 == 0:
            lw = _rwkv_weights(j, norm_w[i], rw_mu, rw_w_rkvz, rw_w0, rw_w1, rw_w2, rw_a0, rw_a1, rw_a2, rw_v0,
                               rw_v1, rw_v2, rw_g1, rw_g2, rw_k_k, rw_k_a, rw_r_k, rw_ln_w, rw_ln_b, rw_w_o)
            xp, sh, sblk, vf_p = _rwkv_layer(xp, jnp.zeros((bp, 1, d), F32), jnp.zeros((bp, RW_PAIRS, LANE, LANE), F32),
                                             vf_p, lw, tile, p_to_seq, p_from_seq, PROMPT_CHUNK)
            p_wkv.append(_unblock_state(sblk))
            p_shift.append(sh[:, 0])
            xs_tm, sh, sblk, vf_s = _rwkv_layer(to_tm(xs), state_shift[j][None], _blockdiag_state(state_wkv[j]),
                                                vf_s, lw, ns, s_to_seq, s_from_seq, SAMPLE_CHUNK)
            xs = from_tm(xs_tm)
            s_wkv.append(_unblock_state(sblk))
            s_shift.append(sh[0])
        else:
            nw = _nsa_weights(j, norm_w[i], nsa_w_in, nsa_q_norm, nsa_k_norm, nsa_cmp_pos, nsa_cmp_w1, nsa_cmp_w2,
                              nsa_w_o)
            xp, newp = _nsa_prompt_layer(xp, nw, toep, cbias, bp, tp, tile)
            xs, news = _nsa_sample_layer(j, xs, nw, caches, wins, page_table, sbias, scbias, ts, past)
            wb = min(WINDOW, tp)
            for m_ in range(4):
                p_kv[m_].append(newp[m_].reshape(bp, tp, NSA_KV, NSA_HD))
                s_kv[m_].append(news[m_].reshape(bs, ts, NSA_KV, NSA_HD))
            for m_ in (4, 5):
                p_kv[m_].append(newp[m_].reshape(bp, tp, NSA_KV, NSA_HD)[:, tp - wb:])
                s_kv[m_].append(news[m_].reshape(bs, win_buf, NSA_KV, NSA_HD))
    st = jnp.stack
    return (xp.reshape(bp, tp, d), xs.reshape(bs, ts, d), st(p_wkv), st(p_shift),
            st(p_kv[0]), st(p_kv[1]), st(p_kv[2]), st(p_kv[3]), st(p_kv[4]), st(p_kv[5]),
            st(s_wkv), st(s_shift),
            st(s_kv[0]), st(s_kv[1]), st(s_kv[2]), st(s_kv[3]), st(s_kv[4]), st(s_kv[5]))
```

```python
import functools
import math

import numpy as np
import jax
import jax.numpy as jnp
from jax import lax
from jax.experimental import pallas as pl
from jax.experimental.pallas import tpu as pltpu

F32 = jnp.float32
BF16 = jnp.bfloat16

D_MODEL = 1024
RW_HEAD = 64
RW_HEADS = 16
RW_PAIRS = 8
GN_EPS = 64e-5
RMS_EPS = 1e-6

NSA_HEADS = 16
NSA_KV = 4
NSA_GROUP = 4
NSA_HD = 64
KV_DIM = 256
CMP_BLOCK = 32
CMP_STRIDE = 16
SEL_BLOCK = 64
SEL_TOPN = 16
WINDOW = 512
NUM_BUCKETS = 32
MAX_DISTANCE = 1024
PAGE = 128

LANE = 128
SEL_PATH_PASSES = 1
VMEM_LIMIT = 48 * 1024 * 1024

NN = ((1,), (0,))
NT = ((1,), (1,))
TN = ((0,), (0,))
NEG_INF = float("-inf")


def _cp(*sem):
    return pltpu.CompilerParams(dimension_semantics=sem, vmem_limit_bytes=VMEM_LIMIT)


def _dg(a, b, dims=NN):
    return lax.dot_general(a, b, (dims, ((), ())), preferred_element_type=F32)


def _split2(x):
    hi = x.astype(BF16)
    lo = (x - hi.astype(F32)).astype(BF16)
    return hi, lo


def _split3(x):
    x0 = x.astype(BF16)
    r1 = x - x0.astype(F32)
    x1 = r1.astype(BF16)
    x2 = (r1 - x1.astype(F32)).astype(BF16)
    return x0, x1, x2


def _dot3s(a, b, dims=NN):
    return _dg(a[0], b[0], dims) + (_dg(a[0], b[1], dims) + _dg(a[1], b[0], dims))


def _dot3(a, b, dims=NN):
    return _dot3s(_split2(a), _split2(b), dims)


def _wdot(x, wh_ref, wl_ref, passes):
    if passes == 1:
        return _dg(x.astype(BF16), wh_ref[...])
    return _dot3s(_split2(x), (wh_ref[...], wl_ref[...]))


def _dot3_stacked(x, wh, wl):
    rows = x.shape[0]
    hi, lo = _split2(x)
    both = _dg(jnp.concatenate([hi, lo], axis=0), wh)
    return both[:rows] + (both[rows:] + _dg(hi, wl))


def _xdot(e, b, dims=NN):
    b0, b1, b2 = _split3(b)
    return _dg(e, b0, dims) + (_dg(e, b1, dims) + _dg(e, b2, dims))


def _sigmoid(x):
    return 1.0 / (1.0 + jnp.exp(-x))


def _softplus(x):
    return jnp.maximum(x, 0.0) + jnp.log(1.0 + jnp.exp(-jnp.abs(x)))


def _rms_rows(x, w_row):
    ms = jnp.mean(x * x, axis=-1, keepdims=True)
    return x * lax.rsqrt(ms + RMS_EPS) * w_row


def _seg_mats(width):
    r = lax.broadcasted_iota(jnp.int32, (width, LANE), 0)
    c = lax.broadcasted_iota(jnp.int32, (width, LANE), 1)
    g = jnp.where(lax.shift_right_logical(r, 6) == c, 1.0, 0.0).astype(BF16)
    r2 = lax.broadcasted_iota(jnp.int32, (LANE, width), 0)
    c2 = lax.broadcasted_iota(jnp.int32, (LANE, width), 1)
    e = jnp.where(lax.shift_right_logical(c2, 6) == r2, 1.0, 0.0).astype(BF16)
    return g, e


def _segsum(x, g, e):
    def two_term(a, w):
        hi, lo = _split2(a)
        return _dg(hi, w) + _dg(lo, w)
    return two_term(two_term(x, g), e)


def _masked_softmax_rows(logits, valid):
    lf = jnp.where(valid, logits, NEG_INF)
    m = jnp.max(lf, axis=-1, keepdims=True)
    m = jnp.where(m == NEG_INF, 0.0, m)
    ex = jnp.exp(lf - m)
    return ex / jnp.maximum(jnp.sum(ex, axis=-1, keepdims=True), 1e-30)


V_NORM, V_MU0, V_W0, V_A0, V_V0, V_KK, V_KA = 0, 1, 7, 8, 9, 10, 11
MU_R, MU_W, MU_K, MU_V, MU_A, MU_G = 0, 1, 2, 3, 4, 5


def _rwkv_proj_kernel(mode, shift, tiles_per_group, has_vres, *refs):
    if mode == "r":
        (x_ref, init_ref, vec_ref, wh_ref, wl_ref, l1h, l1l, l2h, l2l,
         r_ref, lw_ref, sh_ref, carry) = refs
    elif mode == "k":
        (x_ref, init_ref, vec_ref, wh_ref, wl_ref, l1h, l1l, l2h, l2l,
         k_ref, na_ref, b_ref, carry) = refs
    elif mode == "v":
        if has_vres:
            (x_ref, init_ref, vec_ref, wh_ref, wl_ref, l1h, l1l, l2h, l2l, vf_ref,
             v_ref, carry) = refs
        else:
            (x_ref, init_ref, vec_ref, wh_ref, wl_ref, v_ref, carry) = refs
    else:
        (x_ref, init_ref, vec_ref, wh_ref, wl_ref, l1h, l1l, l2h, l2l,
         gz_ref, carry) = refs

    tt = x_ref.shape[0]
    @pl.when((pl.program_id(0) % tiles_per_group) == 0)
    def _():
        carry[...] = init_ref[...]

    h = _rms_rows(x_ref[...], vec_ref[V_NORM:V_NORM + 1, :])
    prev_rows = carry[...]
    if shift == 1:
        rolled = pltpu.roll(h, 1, axis=0)
        rowi = lax.broadcasted_iota(jnp.int32, (tt, 1), 0)
        prev = jnp.where(rowi == 0, prev_rows, rolled)
    else:
        prev = jnp.concatenate([prev_rows, h[:tt - shift, :]], axis=0)
    carry[...] = h[tt - shift:, :]
    xx = prev - h

    def mix(i):
        return h + xx * vec_ref[V_MU0 + i:V_MU0 + i + 1, :]

    def big(xm):
        return _wdot(xm, wh_ref, wl_ref, 1)

    def lora(xm, act):
        return _wdot(act(_wdot(xm, l1h, l1l, 1)), l2h, l2l, 1)

    if mode == "r":
        r_ref[...] = big(mix(MU_R))
        wraw = vec_ref[V_W0:V_W0 + 1, :] + lora(mix(MU_W), jnp.tanh)
        w = -_softplus(-wraw) - 0.5
        lw_ref[...] = -jnp.exp(w)
        sh_ref[...] = h[tt - shift:, :]
    elif mode == "k":
        kraw = big(mix(MU_K))
        a = _sigmoid(vec_ref[V_A0:V_A0 + 1, :] + lora(mix(MU_A), lambda t: t))
        g, e = _seg_mats(D_MODEL)
        kk = kraw * vec_ref[V_KK:V_KK + 1, :]
        ss = _segsum(kk * kk, g, e)
        kk = kk / jnp.maximum(jnp.sqrt(ss), 1e-12)
        k_ref[...] = kraw * (1.0 + (a - 1.0) * vec_ref[V_KA:V_KA + 1, :])
        na_ref[...] = -kk
        b_ref[...] = kk * a
    elif mode == "v":
        xm = mix(MU_V)
        v = big(xm)
        if has_vres:
            vg = _sigmoid(vec_ref[V_V0:V_V0 + 1, :] + lora(xm, lambda t: t))
            v = v + (vf_ref[...] - v) * vg
        v_ref[...] = v
    else:
        xm = mix(MU_G)
        z = big(xm)
        gate = lora(xm, _sigmoid)
        gz_ref[...] = gate * (z * _sigmoid(z))


def _rwkv_proj(mode, x2d, init_prev, vecs, w_pair, lora_w, vfirst, tile):
    n = x2d.shape[0]
    groups, shift, _ = init_prev.shape
    tiles_per_group = (n // groups) // tile
    has_vres = mode == "v" and lora_w is not None
    row_spec = pl.BlockSpec((tile, D_MODEL), lambda i: (i, 0))
    init_spec = pl.BlockSpec((None, shift, D_MODEL), lambda i: (i // tiles_per_group, 0, 0))
    full = lambda a: pl.BlockSpec(a.shape, lambda i: (0,) * a.ndim)
    ins = [x2d, init_prev, vecs, w_pair[0], w_pair[1]]
    in_specs = [row_spec, init_spec, full(vecs), full(w_pair[0]), full(w_pair[1])]
    if lora_w is not None:
        ins += list(lora_w)
        in_specs += [full(a) for a in lora_w]
    if has_vres:
        ins.append(vfirst)
        in_specs.append(row_spec)
    act = jax.ShapeDtypeStruct((n, D_MODEL), F32)
    if mode == "r":
        out_shape = (act, act, jax.ShapeDtypeStruct(init_prev.shape, F32))
        out_specs = (row_spec, row_spec, init_spec)
    elif mode == "k":
        out_shape = (act, act, act)
        out_specs = (row_spec, row_spec, row_spec)
    else:
        out_shape = act
        out_specs = row_spec
    return pl.pallas_call(
        functools.partial(_rwkv_proj_kernel, mode, shift, tiles_per_group, has_vres),
        grid=(n // tile,),
        in_specs=in_specs,
        out_specs=out_specs,
        out_shape=out_shape,
        scratch_shapes=[pltpu.VMEM((shift, D_MODEL), F32)],
        compiler_params=_cp("arbitrary"),
        name="rwkv_proj_" + mode,
    )(*ins)


def _wkv_pairs(chunk, n_dbl, r, lw, kvec, v, na, bvec, s_old):
    ri = lax.broadcasted_iota(jnp.int32, (chunk, chunk), 0)
    ci = lax.broadcasted_iota(jnp.int32, (chunk, chunk), 1)
    tri = jnp.where(ci <= ri, 1.0, 0.0).astype(BF16)
    idx = range(len(r))
    each = lambda fn: [fn(i) for i in idx]
    cum = each(lambda i: _xdot(tri, lw[i]))
    total = each(lambda i: cum[i][chunk - 1:chunk, :])
    w_in = each(lambda i: jnp.exp(cum[i]))
    w_ex = each(lambda i: jnp.exp(cum[i] - lw[i]))
    w_inv = each(lambda i: jnp.exp(-cum[i]))
    w_rest = each(lambda i: jnp.exp(total[i] - cum[i]))

    lane = lax.broadcasted_iota(jnp.int32, (chunk, LANE), 1)
    m0 = lane < RW_HEAD

    def stack(x):
        return jnp.concatenate([jnp.where(m0, x, 0.0), jnp.where(m0, 0.0, x)], axis=0)

    b16 = lambda t: t.astype(BF16)
    a_s = each(lambda i: b16(stack(na[i] * w_ex[i])))
    r_s = each(lambda i: b16(stack(r[i] * w_in[i])))
    b_s = each(lambda i: b16(stack(bvec[i] * w_inv[i])))
    k_s = each(lambda i: b16(stack(kvec[i] * w_inv[i])))
    v_s = each(lambda i: b16(stack(v[i])))
    bh_s = each(lambda i: b16(stack(bvec[i] * w_rest[i])))
    kh_s = each(lambda i: b16(stack(kvec[i] * w_rest[i])))
    s_sp = each(lambda i: b16(s_old[i]))

    c2 = 2 * chunk
    r2 = lax.broadcasted_iota(jnp.int32, (c2, c2), 0)
    q2 = lax.broadcasted_iota(jnp.int32, (c2, c2), 1)
    strict = q2 < r2
    incl = q2 <= r2
    l_ab = each(lambda i: jnp.where(strict, _dg(a_s[i], b_s[i], NT), 0.0))
    l_ak = each(lambda i: jnp.where(strict, _dg(a_s[i], k_s[i], NT), 0.0))
    m_rb = each(lambda i: jnp.where(incl, _dg(r_s[i], b_s[i], NT), 0.0))
    m_rk = each(lambda i: jnp.where(incl, _dg(r_s[i], k_s[i], NT), 0.0))

    x = each(lambda i: _dg(a_s[i], s_sp[i], NT) + _dg(b16(l_ak[i]), v_s[i]))
    p = l_ab
    for it in range(n_dbl):
        p_sp = each(lambda i: b16(p[i]))
        x = each(lambda i: x[i] + _dg(p_sp[i], b16(x[i])))
        if it < n_dbl - 1:
            p = each(lambda i: _dg(p_sp[i], p_sp[i]))
    u_s = each(lambda i: b16(x[i]))
    y = each(lambda i: _dg(r_s[i], s_sp[i], NT) + _dg(b16(m_rb[i]), u_s[i]) + _dg(b16(m_rk[i]), v_s[i]))
    s_new = each(lambda i: s_old[i] * jnp.exp(total[i])
                 + (_dg(u_s[i], bh_s[i], TN) + _dg(v_s[i], kh_s[i], TN)))
    return each(lambda i: y[i][:chunk, :] + y[i][chunk:, :]), s_new


WKV_PAIRS_PER_STEP = 8


def _wkv_kernel(chunk, n_dbl, r_ref, lw_ref, k_ref, v_ref, na_ref, b_ref, s0_ref,
                y_ref, sout_ref, s_scr):
    c = pl.program_id(2)

    @pl.when(c == 0)
    def _():
        s_scr[...] = s0_ref[...]

    pairs = range(WKV_PAIRS_PER_STEP)
    cols = lambda ref: [ref[:, pp * LANE:(pp + 1) * LANE] for pp in pairs]
    y, s_new = _wkv_pairs(chunk, n_dbl, cols(r_ref), cols(lw_ref), cols(k_ref), cols(v_ref), cols(na_ref),
                          cols(b_ref), [s_scr[pp] for pp in pairs])
    for pp in pairs:
        y_ref[:, pp * LANE:(pp + 1) * LANE] = y[pp]
        s_scr[pp] = s_new[pp]

    @pl.when(c == pl.num_programs(2) - 1)
    def _():
        sout_ref[...] = s_scr[...]


def _wkv(r, lw, k, v, na, b, s0blk, chunk):
    g, t, _ = r.shape
    n_dbl = int(round(math.log2(chunk)))
    pps = WKV_PAIRS_PER_STEP
    seq_spec = pl.BlockSpec((None, chunk, pps * LANE), lambda gi, p, c: (gi, c, p))
    st_spec = pl.BlockSpec((None, pps, LANE, LANE), lambda gi, p, c: (gi, p, 0, 0))
    return pl.pallas_call(
        functools.partial(_wkv_kernel, chunk, n_dbl),
        grid=(g, RW_PAIRS // pps, t // chunk),
        in_specs=[seq_spec] * 6 + [st_spec],
        out_specs=(seq_spec, st_spec),
        out_shape=(jax.ShapeDtypeStruct(r.shape, F32), jax.ShapeDtypeStruct(s0blk.shape, F32)),
        scratch_shapes=[pltpu.VMEM((pps, LANE, LANE), F32)],
        compiler_params=_cp("arbitrary", "arbitrary", "arbitrary"),
        name="wkv_scan",
    )(r, lw, k, v, na, b, s0blk)


def _rwkv_out_kernel(y_ref, r_ref, k_ref, v_ref, gz_ref, x_ref, vec_ref, wh_ref, wl_ref, o_ref):
    g, e = _seg_mats(D_MODEL)
    y = y_ref[...]
    inv = 1.0 / RW_HEAD
    mean = _segsum(y, g, e) * inv
    d = y - mean
    var = _segsum(d * d, g, e) * inv
    yn = d * lax.rsqrt(var + GN_EPS) * vec_ref[1:2, :] + vec_ref[2:3, :]
    v = v_ref[...]
    bonus = _segsum(r_ref[...] * k_ref[...] * vec_ref[0:1, :], g, e) * v
    o = (yn + bonus) * gz_ref[...]
    o_ref[...] = x_ref[...] + _wdot(o, wh_ref, wl_ref, 1)


def _rwkv_out(y, r, k, v, gz, x2d, vecs, w_pair, tile):
    n = x2d.shape[0]
    row_spec = pl.BlockSpec((tile, D_MODEL), lambda i: (i, 0))
    full = lambda a: pl.BlockSpec(a.shape, lambda i: (0,) * a.ndim)
    return pl.pallas_call(
        _rwkv_out_kernel,
        grid=(n // tile,),
        in_specs=[row_spec] * 6 + [full(vecs), full(w_pair[0]), full(w_pair[1])],
        out_specs=row_spec,
        out_shape=jax.ShapeDtypeStruct((n, D_MODEL), F32),
        compiler_params=_cp("arbitrary"),
        name="rwkv_out",
    )(y, r, k, v, gz, x2d, vecs, w_pair[0], w_pair[1])


def _wsplit(w):
    hi = w.astype(BF16)
    lo = (w - hi.astype(F32)).astype(BF16)
    return hi, lo


def _pad_to(a, axis, size):
    pad = [(0, 0)] * a.ndim
    pad[axis] = (0, size - a.shape[axis])
    return jnp.pad(a, pad)


def _lora_pair(w1, w2):
    rank = w1.shape[1]
    rp = ((rank + LANE - 1) // LANE) * LANE
    return _wsplit(_pad_to(w1, 1, rp)) + _wsplit(_pad_to(w2, 0, rp))


def _blockdiag_state(s):
    g = s.shape[0]
    s = s.reshape(g, RW_PAIRS, 2, RW_HEAD, RW_HEAD)
    z = jnp.zeros_like(s[:, :, 0])
    top = jnp.concatenate([s[:, :, 0], z], axis=-1)
    bot = jnp.concatenate([z, s[:, :, 1]], axis=-1)
    return jnp.concatenate([top, bot], axis=-2)


def _unblock_state(sb):
    g = sb.shape[0]
    a = sb[:, :, :RW_HEAD, :RW_HEAD]
    b = sb[:, :, RW_HEAD:, RW_HEAD:]
    return jnp.stack([a, b], axis=2).reshape(g, RW_HEADS, RW_HEAD, RW_HEAD)


def _rwkv_layer(x2d, init_prev, s0blk, vfirst, lw, tile, to_seq, from_seq, chunk):
    vecs, w_r, w_k, w_v, w_z, lo_w, lo_a, lo_v, lo_g, vecs_out, w_o = lw
    r, logw, shift_out = _rwkv_proj("r", x2d, init_prev, vecs, w_r, lo_w, None, tile)
    k, na, b = _rwkv_proj("k", x2d, init_prev, vecs, w_k, lo_a, None, tile)
    v = _rwkv_proj("v", x2d, init_prev, vecs, w_v, lo_v, vfirst, tile)
    gz = _rwkv_proj("z", x2d, init_prev, vecs, w_z, lo_g, None, tile)
    if vfirst is None:
        vfirst = v
    y, sblk = _wkv(*(to_seq(t) for t in (r, logw, k, v, na, b)), s0blk, chunk)
    x_new = _rwkv_out(from_seq(y), r, k, v, gz, x2d, vecs_out, w_o, tile)
    return x_new, shift_out, sblk, vfirst


def _rwkv_weights(j, norm_w_i, rw_mu, rw_w_rkvz, rw_w0, rw_w1, rw_w2, rw_a0, rw_a1, rw_a2, rw_v0, rw_v1,
                  rw_v2, rw_g1, rw_g2, rw_k_k, rw_k_a, rw_r_k, rw_ln_w, rw_ln_b, rw_w_o):
    zero = jnp.zeros((D_MODEL,), F32)
    v0 = rw_v0[j - 1] if j > 0 else zero
    vecs = jnp.stack([norm_w_i] + [rw_mu[j, i] for i in range(6)]
                     + [rw_w0[j], rw_a0[j], v0, rw_k_k[j], rw_k_a[j]] + [zero] * 4)
    lo_v = _lora_pair(rw_v1[j - 1], rw_v2[j - 1]) if j > 0 else None
    vecs_out = jnp.stack([rw_r_k[j].reshape(D_MODEL), rw_ln_w[j], rw_ln_b[j]] + [zero] * 5)
    return (vecs, _wsplit(rw_w_rkvz[j, 0]), _wsplit(rw_w_rkvz[j, 1]), _wsplit(rw_w_rkvz[j, 2]),
            _wsplit(rw_w_rkvz[j, 3]), _lora_pair(rw_w1[j], rw_w2[j]), _lora_pair(rw_a1[j], rw_a2[j]),
            lo_v, _lora_pair(rw_g1[j], rw_g2[j]), vecs_out, _wsplit(rw_w_o[j]))


def _seg_rms(y, aux_row, width):
    g, e = _seg_mats(width)
    ms = _segsum(y * y, g, e) * (1.0 / NSA_HD)
    return y * lax.rsqrt(ms + RMS_EPS) * aux_row


def _nsa_proj_kernel(mode, x_ref, nw_ref, wh_ref, wl_ref, aux_ref, *out_refs):
    h = _rms_rows(x_ref[...], nw_ref[...])
    y = _wdot(h, wh_ref, wl_ref, SEL_PATH_PASSES if mode in ("q", "kv") else 1)
    if mode == "q":
        out_refs[0][...] = _seg_rms(y, aux_ref[0:1, :], D_MODEL) * (NSA_HD ** -0.5)
    elif mode == "kv":
        kc, vc, ks, vs, kw, vw = (y[:, i * KV_DIM:(i + 1) * KV_DIM] for i in range(6))
        out_refs[0][...] = kc
        out_refs[1][...] = vc
        out_refs[2][...] = _seg_rms(ks, aux_ref[0:1, :], KV_DIM)
        out_refs[3][...] = vs
        out_refs[4][...] = _seg_rms(kw, aux_ref[1:2, :], KV_DIM)
        out_refs[5][...] = vw
    elif mode == "z":
        out_refs[0][...] = y * _sigmoid(y)
    else:
        out_refs[0][...] = _sigmoid(y)


def _nsa_proj(mode, x2d, nw_row, w_pair, aux, tile):
    n = x2d.shape[0]
    width = w_pair[0].shape[1]
    row_spec = pl.BlockSpec((tile, D_MODEL), lambda i: (i, 0))
    full = lambda a: pl.BlockSpec(a.shape, lambda i: (0,) * a.ndim)
    if mode == "kv":
        out_shape = tuple(jax.ShapeDtypeStruct((n, KV_DIM), F32) for _ in range(6))
        out_specs = tuple(pl.BlockSpec((tile, KV_DIM), lambda i: (i, 0)) for _ in range(6))
    else:
        out_shape = (jax.ShapeDtypeStruct((n, width), F32),)
        out_specs = (pl.BlockSpec((tile, width), lambda i: (i, 0)),)
    res = pl.pallas_call(
        functools.partial(_nsa_proj_kernel, mode),
        grid=(n // tile,),
        in_specs=[row_spec, full(nw_row), full(w_pair[0]), full(w_pair[1]), full(aux)],
        out_specs=out_specs,
        out_shape=out_shape,
        compiler_params=_cp("arbitrary"),
        name="nsa_proj_" + mode,
    )(x2d, nw_row, w_pair[0], w_pair[1], aux)
    return res if mode == "kv" else res[0]


def _compress_kernel(n_prefetch, n_src, m, do_norm, transposed, *refs):
    refs = refs[n_prefetch:]
    pos_ref, wah, wal, wbh, wbl, w2h, w2l, nrm_ref, o_ref = refs[(n_src if transposed else 2 * n_src):]
    rows_per_src = m // n_src
    acc_a = jnp.zeros((m, KV_DIM), F32)
    acc_b = jnp.zeros((m, KV_DIM), F32)

    if transposed:
        pr = lax.broadcasted_iota(jnp.int32, (PAGE, PAGE), 0)
        pc = lax.broadcasted_iota(jnp.int32, (PAGE, PAGE), 1)
        perm = jnp.where(pc == (pr % rows_per_src) * CMP_STRIDE + pr // rows_per_src, 1.0, 0.0).astype(BF16)
        pages = []
        for s in refs[:n_src]:
            hi, lo = _split2(s[...])
            pages.append(_dg(perm, hi, NT) + _dg(perm, lo, NT))

        def rows_of(jj):
            return jnp.concatenate([p[jj * rows_per_src:(jj + 1) * rows_per_src, :] for p in pages], axis=0)
    else:
        src_lo = refs[:n_src]
        src_hi = refs[n_src:2 * n_src]

        def strided(srcs, jj):
            parts = [s[pl.ds(jj, rows_per_src, stride=CMP_STRIDE), :] for s in srcs]
            return parts[0] if n_src == 1 else jnp.concatenate(parts, axis=0)

        def rows_of(jj):
            return jnp.concatenate([strided(src_lo, jj), strided(src_hi, jj)], axis=-1)

    for jj in range(CMP_STRIDE):
        x = rows_of(jj)
        xa = x + pos_ref[jj:jj + 1, :]
        xb = x + pos_ref[CMP_STRIDE + jj:CMP_STRIDE + jj + 1, :]
        if do_norm and SEL_PATH_PASSES == 3:
            acc_a = acc_a + _dot3_stacked(xa, wah[jj], wal[jj])
            acc_b = acc_b + _dot3_stacked(xb, wbh[jj], wbl[jj])
        else:
            acc_a = acc_a + _dg(xa.astype(BF16), wah[jj])
            acc_b = acc_b + _dg(xb.astype(BF16), wbh[jj])
    pre = acc_a + pltpu.roll(acc_b, m - 1, axis=0)
    hid = pre * _sigmoid(pre)
    out = _dot3s(_split2(hid), (w2h[...], w2l[...]))
    if do_norm:
        out = _seg_rms(out, nrm_ref[0:1, :], KV_DIM)
    o_ref[...] = out


def _compress_call(srcs, half_specs, grid, n_prefetch, prefetch, cw, do_norm, m, batch, idx, transposed=False):
    pos, wa, wb, w2, nrm = cw
    consts = [pos, wa[0], wa[1], wb[0], wb[1], w2[0], w2[1], nrm]
    cspecs = [pl.BlockSpec(a.shape, functools.partial(lambda nd, *_: (0,) * nd, a.ndim)) for a in consts]
    out_spec = pl.BlockSpec((None, m, KV_DIM), idx)
    src_specs = half_specs(None) if transposed else half_specs(0) + half_specs(1)
    src_args = list(srcs) if transposed else list(srcs) + list(srcs)
    gs = pltpu.PrefetchScalarGridSpec(
        num_scalar_prefetch=n_prefetch, grid=grid, in_specs=src_specs + cspecs, out_specs=out_spec)
    return pl.pallas_call(
        functools.partial(_compress_kernel, n_prefetch, len(srcs), m, do_norm, transposed),
        grid_spec=gs,
        out_shape=jax.ShapeDtypeStruct((batch, m, KV_DIM), F32),
        compiler_params=_cp("arbitrary"),
        name="nsa_compress",
    )(*prefetch, *src_args, *consts)


def _compress_weights(pos, w1, w2, knorm_row):
    eye = jnp.eye(NSA_KV, dtype=F32)
    w1r = w1.reshape(CMP_BLOCK, NSA_HD, NSA_HD)
    blk = jnp.einsum("ab,jde->jadbe", eye, w1r).reshape(CMP_BLOCK, KV_DIM, KV_DIM)
    w2b = jnp.einsum("ab,de->adbe", eye, w2).reshape(KV_DIM, KV_DIM)
    nrm = jnp.zeros((8, KV_DIM), F32).at[0].set(jnp.tile(knorm_row, NSA_KV))
    return (jnp.tile(pos, (1, NSA_KV)), _wsplit(blk[:CMP_STRIDE]), _wsplit(blk[CMP_STRIDE:]), _wsplit(w2b), nrm)


def _bucket_thresholds():
    n = np.arange(0, 8192, dtype=np.float64)
    max_exact = NUM_BUCKETS // 2
    large = max_exact + np.floor(
        np.log(np.maximum(n, 1.0) / max_exact) / math.log(MAX_DISTANCE / max_exact) * (NUM_BUCKETS - max_exact))
    bucket = np.where(n < max_exact, n, np.minimum(large, NUM_BUCKETS - 1)).astype(np.int64)
    return [int(np.argmax(bucket >= m)) for m in range(NUM_BUCKETS)]


def _bias_kernel(thr, base_mul, base_add, col_stride, tab_ref, o_ref):
    i = pl.program_id(0)
    h = pl.program_id(1)
    rows, cols = o_ref.shape
    r = lax.broadcasted_iota(jnp.int32, (rows, cols), 0)
    c = lax.broadcasted_iota(jnp.int32, (rows, cols), 1)
    d = (i * base_mul + base_add) + r - c * col_stride
    val = jnp.full((rows, cols), tab_ref[0, h], F32)
    for m in range(1, NUM_BUCKETS):
        val = jnp.where(d >= thr[m], tab_ref[m, h], val)
    o_ref[...] = val


def _bias_table(rel_bias, n_i, rows, cols, base_mul, base_add, col_stride):
    return pl.pallas_call(
        functools.partial(_bias_kernel, _bucket_thresholds(), base_mul, base_add, col_stride),
        grid=(n_i, NSA_HEADS),
        in_specs=[pl.BlockSpec(memory_space=pltpu.SMEM)],
        out_specs=pl.BlockSpec((None, None, rows, cols), lambda i, h: (i, h, 0, 0)),
        out_shape=jax.ShapeDtypeStruct((n_i, NSA_HEADS, rows, cols), F32),
        compiler_params=_cp("arbitrary", "arbitrary"),
        name="nsa_bias_table",
    )(rel_bias)


def _overlap_t(n_cmp, n_cmp_pad, n_sel, n_sel_pad):
    cs = np.arange(n_cmp_pad)[None, :] * CMP_STRIDE
    ss = np.arange(n_sel_pad)[:, None] * SEL_BLOCK
    ov = np.maximum(np.minimum(cs + CMP_BLOCK, ss + SEL_BLOCK) - np.maximum(cs, ss), 0) / CMP_BLOCK
    ov = ov * (np.arange(n_cmp_pad)[None, :] < n_cmp) * (np.arange(n_sel_pad)[:, None] < n_sel)
    return jnp.asarray(ov, F32)


def _online_update_b16(carry, q_b, k_tiles, v_tiles, add_tiles, mk):
    m, acc = carry
    lfs = [_dg(q_b, k.astype(BF16), NT).astype(BF16) + a for k, a in zip(k_tiles, add_tiles)]
    emax = functools.reduce(jnp.maximum, lfs).astype(F32)
    m_new = jnp.maximum(m, jnp.max(emax, axis=-1, keepdims=True))
    m_safe = jnp.where(m_new == NEG_INF, 0.0, m_new)
    alpha = jnp.exp(m - m_safe)
    m_b = jnp.broadcast_to(m_safe, lfs[0].shape).astype(BF16)
    pv = functools.reduce(jnp.add, [_dg(jnp.exp(lf - m_b), jnp.where(mk, v, 1.0).astype(BF16))
                                    for lf, v in zip(lfs, v_tiles)])
    return m_new, alpha * acc + pv


def _attn_init_b16(rows):
    return jnp.full((rows, 1), NEG_INF, F32), jnp.zeros((rows, LANE), F32)


def _attn_finish_b16(carry, mk):
    _, acc = carry
    denom = jnp.max(jnp.where(mk, 0.0, acc), axis=-1, keepdims=True)
    return acc / jnp.maximum(denom, 1e-30)


def _attn_finish(carry):
    _, l, acc = carry
    return acc / jnp.maximum(l, 1e-30)


def _select_blocks(imp_t, qpos_lane, sc_scr):
    n_sel = imp_t.shape[0]
    sidx = lax.broadcasted_iota(jnp.int32, imp_t.shape, 0)
    cur = lax.shift_right_logical(qpos_lane, 6)
    future = sidx * SEL_BLOCK > qpos_lane
    forced = (sidx == 0) | (sidx == cur) | (sidx == cur - 1)
    score = jnp.where(future, NEG_INF, jnp.where(forced, float("inf"), imp_t))
    sc_scr[...] = score

    def body(s2, cnt):
        rowv = sc_scr[pl.ds(s2, 1), :]
        ahead = (rowv > score) | ((rowv == score) & (s2 < sidx))
        return cnt + jnp.where(ahead, 1.0, 0.0)

    cnt = lax.fori_loop(0, n_sel, body, jnp.zeros(imp_t.shape, F32), unroll=4)
    return jnp.where(cnt < SEL_TOPN, 1.0, 0.0).astype(BF16)


def _block_mask(sel_t, kt):
    n_sel = sel_t.shape[0]
    sr = lax.broadcasted_iota(jnp.int32, (n_sel, LANE), 0)
    kc = lax.broadcasted_iota(jnp.int32, (n_sel, LANE), 1)
    expand = jnp.where(sr == 2 * kt + lax.shift_right_logical(kc, 6), 1.0, 0.0).astype(BF16)
    return _dg(sel_t, expand, TN)


def _lane_mask(par, rows):
    lane = lax.broadcasted_iota(jnp.int32, (rows, LANE), 1)
    lo = par * NSA_HD
    return (lane >= lo) & (lane < lo + NSA_HD)


SEL_GROUP = 8


def _nsa_prompt_kernel(n_sel, mc, *refs):
    q_refs = refs[0:4]
    gate_ref, kc_ref, vc_ref, ks_ref, vs_ref = refs[4:9]
    kw_refs = refs[9:14]
    vw_refs = refs[14:19]
    toep_ref, cb_ref, ovt_ref = refs[19:22]
    o_refs = refs[22:26]
    sc_scr = refs[26]

    qb = pl.program_id(1)
    par = pl.program_id(2) % 2
    mk = _lane_mask(par, LANE)
    q = jnp.concatenate([jnp.where(mk, r[...], 0.0) for r in q_refs], axis=0)
    q_sp = _split2(q)
    rows = NSA_GROUP * LANE
    qpos = qb * LANE + (lax.broadcasted_iota(jnp.int32, (rows, 1), 0) & (LANE - 1))

    qrow = qb * LANE + lax.broadcasted_iota(jnp.int32, (LANE, LANE), 0)
    kcol = lax.broadcasted_iota(jnp.int32, (LANE, LANE), 1)
    neg = jnp.full((LANE, LANE), NEG_INF, BF16)
    mk2 = _lane_mask(par, 2 * LANE)
    mk_rows = _lane_mask(par, rows)
    res = {}

    def add_tile(ok, ti):
        return jnp.concatenate([jnp.where(ok, toep_ref[ti, g], neg) for g in range(NSA_GROUP)], axis=0)

    def compressed_chain():
        cend = lax.broadcasted_iota(jnp.int32, (1, mc), 1) * CMP_STRIDE + (CMP_BLOCK - 1)
        if SEL_PATH_PASSES == 1:
            qk_c = _dg(q_sp[0], kc_ref[...].astype(BF16), NT)
        else:
            qk_c = _dot3s(q_sp, _split2(kc_ref[...]), NT)
        yield
        lc = qk_c + jnp.concatenate([cb_ref[g] for g in range(NSA_GROUP)], axis=0)
        p_c = _masked_softmax_rows(lc, cend <= qpos)
        yield
        res["o_c"] = _dg(p_c.astype(BF16), vc_ref[...].astype(BF16))
        psum = (p_c[0:LANE] + p_c[LANE:2 * LANE]) + (p_c[2 * LANE:3 * LANE] + p_c[3 * LANE:4 * LANE])
        res["imp_t"] = _dot3(ovt_ref[...], psum, NT)

    def window_chain():
        def win_add(i):
            kpos = (qb - 4 + i) * LANE + kcol
            dist = qrow - kpos
            return add_tile((kpos >= 0) & (dist >= 0) & (dist < WINDOW), 4 - i)

        pair = lambda refs_, i: jnp.concatenate([refs_[i][...], refs_[i + 1][...]], axis=0)
        adds = [jnp.concatenate([win_add(0), win_add(1)], axis=1), jnp.concatenate([win_add(2), win_add(3)], axis=1)]
        yield
        carry = _online_update_b16(_attn_init_b16(rows), q_sp[0], [pair(kw_refs, 0), pair(kw_refs, 2)],
                                   [pair(vw_refs, 0), pair(vw_refs, 2)], adds, mk2)
        yield
        carry = _online_update_b16(carry, q_sp[0], [kw_refs[4][...]], [vw_refs[4][...]], [win_add(4)], mk)
        res["o_w"] = _attn_finish_b16(carry, mk_rows)

    _round_robin([compressed_chain(), window_chain()])
    o_c, o_w = res["o_c"], res["o_w"]
    qpos_lane = qb * LANE + lax.broadcasted_iota(jnp.int32, (n_sel, LANE), 1)
    sel_t = _select_blocks(res["imp_t"], qpos_lane, sc_scr)

    def sel_add(kt):
        ok = (_block_mask(sel_t, kt) > 0.5) & (kt * LANE + kcol <= qrow)
        return add_tile(ok, jnp.clip(qb - kt, 0, 8))

    def sel_body(it, carry):
        k_tiles, v_tiles, adds = [], [], []
        for i in range(0, SEL_GROUP, 2):
            kt = it * SEL_GROUP + i
            start = pl.multiple_of(kt * LANE, 2 * LANE)
            adds.append(jnp.concatenate([sel_add(kt), sel_add(kt + 1)], axis=1))
            k_tiles.append(ks_ref[pl.ds(start, 2 * LANE), :])
            v_tiles.append(vs_ref[pl.ds(start, 2 * LANE), :])
        return _online_update_b16(carry, q_sp[0], k_tiles, v_tiles, adds, mk2)

    n_it = (qb + SEL_GROUP) // SEL_GROUP
    o_s = _attn_finish_b16(lax.fori_loop(0, n_it, sel_body, _attn_init_b16(rows)), mk_rows)

    for g in range(NSA_GROUP):
        sl = slice(g * LANE, (g + 1) * LANE)
        og = (gate_ref[:, 3 * g:3 * g + 1] * o_c[sl] + gate_ref[:, 3 * g + 1:3 * g + 2] * o_s[sl]
              + gate_ref[:, 3 * g + 2:3 * g + 3] * o_w[sl])
        og = jnp.where(mk, og, 0.0)
        o_ref = o_refs[g]

        @pl.when(par == 0)
        def _():
            o_ref[...] = og

        @pl.when(par == 1)
        def _():
            o_ref[...] = o_ref[...] + og


def _nsa_prompt_attn(qp, gates, kcmp, vcmp, ks, vs, kw, vw, toep, cbias, batch, seq):
    nqb = seq // LANE
    mc = seq // CMP_STRIDE
    n_sel = seq // SEL_BLOCK
    ovt = _overlap_t(mc - 1, mc, n_sel, n_sel)
    q_specs = [pl.BlockSpec((LANE, LANE), functools.partial(lambda g, b, qb, kh: (b * nqb + qb, 2 * g + kh // 2), g))
               for g in range(NSA_GROUP)]
    half = lambda rows: pl.BlockSpec((None, rows, LANE), lambda b, qb, kh: (b, 0, kh // 2))
    win_specs = [pl.BlockSpec((None, LANE, LANE),
                              functools.partial(lambda i, b, qb, kh: (b, jnp.maximum(qb - 4 + i, 0), kh // 2), i))
                 for i in range(5)]
    in_specs = (q_specs
                + [pl.BlockSpec((LANE, LANE), lambda b, qb, kh: (b * nqb + qb, kh))]
                + [half(mc), half(mc), half(seq), half(seq)]
                + win_specs + win_specs
                + [pl.BlockSpec((9, NSA_GROUP, LANE, LANE), lambda b, qb, kh: (0, kh, 0, 0)),
                   pl.BlockSpec((None, NSA_GROUP, LANE, mc), lambda b, qb, kh: (qb, kh, 0, 0)),
                   pl.BlockSpec(ovt.shape, lambda b, qb, kh: (0, 0))])
    out_spec = pl.BlockSpec((LANE, LANE), lambda b, qb, kh: (b * nqb + qb, kh // 2))
    n = batch * seq
    return pl.pallas_call(
        functools.partial(_nsa_prompt_kernel, n_sel, mc),
        grid=(batch, nqb, NSA_KV),
        in_specs=in_specs,
        out_specs=(out_spec,) * NSA_GROUP,
        out_shape=tuple(jax.ShapeDtypeStruct((n, KV_DIM), F32) for _ in range(NSA_GROUP)),
        scratch_shapes=[pltpu.VMEM((n_sel, LANE), F32)],
        compiler_params=_cp("arbitrary", "arbitrary", "arbitrary"),
        name="nsa_prompt_attn",
    )(qp, qp, qp, qp, gates, kcmp, vcmp, ks, vs, *([kw] * 5), *([vw] * 5), toep.astype(BF16), cbias, ovt)


T_PAD = 8
SAMPLE_GROUP = 8


def _round_robin(gens):
    active = list(gens)
    while active:
        still = []
        for gen in active:
            try:
                next(gen)
                still.append(gen)
            except StopIteration:
                pass
        active = still


def _online_update_wide(carry, q_sp, k_ops, v_ops, add_tiles):
    m, l, acc = carry
    lfs = [_dot3s(q_sp, k, NN if t else NT) + a for (k, t), a in zip(k_ops, add_tiles)]
    yield
    emax = functools.reduce(jnp.maximum, lfs)
    m_new = jnp.maximum(m, jnp.max(emax, axis=-1, keepdims=True))
    m_safe = jnp.where(m_new == NEG_INF, 0.0, m_new)
    alpha = jnp.exp(m - m_safe)
    ps = [jnp.exp(lf - m_safe) for lf in lfs]
    l = alpha * l + jnp.sum(functools.reduce(jnp.add, ps), axis=-1, keepdims=True)
    yield
    pv = functools.reduce(jnp.add, [_dg(p.astype(BF16), v, NT if t else NN) for p, (v, t) in zip(ps, v_ops)])
    yield
    return m_new, l, alpha * acc + pv


def _nsa_decode_kernel_t(past, dec_seq, n_pages, win_buf, n_sel_pad, *refs):
    n = n_pages
    q_ref, gate_ref, kc_ref, vc_ref = refs[1:5]
    ks_pages = refs[5:5 + n]
    vs_pages = refs[5 + n:5 + 2 * n]
    ksn, vsn, kwn, vwn, wk_ref, wv_ref, sb_ref, cb_ref, ovt_ref = refs[5 + 2 * n:14 + 2 * n]
    o_refs = refs[14 + 2 * n:18 + 2 * n]
    wko_ref, wvo_ref = refs[18 + 2 * n:20 + 2 * n]

    rows = NSA_GROUP * T_PAD
    qpos = past + (lax.broadcasted_iota(jnp.int32, (rows, 1), 0) & (T_PAD - 1))
    qpos8 = past + lax.broadcasted_iota(jnp.int32, (T_PAD, LANE), 0)
    kcol = lax.broadcasted_iota(jnp.int32, (T_PAD, LANE), 1)
    zpad = jnp.zeros((LANE - T_PAD, KV_DIM), F32)
    zpad_t = jnp.zeros((LANE - T_PAD, LANE), F32)
    mc = kc_ref.shape[0]
    cend = lax.broadcasted_iota(jnp.int32, (1, mc), 1) * CMP_STRIDE + (CMP_BLOCK - 1)
    qpos_lane = past + lax.broadcasted_iota(jnp.int32, (n_sel_pad, LANE), 1)
    sidx = lax.broadcasted_iota(jnp.int32, (n_sel_pad, LANE), 0)
    always = lambda kpos: kpos >= 0
    in_win = lambda kpos: (qpos8 - kpos) < WINDOW
    w0 = past - win_buf
    n_wt = win_buf // LANE
    lane256 = lax.broadcasted_iota(jnp.int32, (T_PAD, KV_DIM), 1)

    def per_group(fn):
        return jnp.concatenate([fn(g) for g in range(NSA_GROUP)], axis=0)

    shared = {}

    def k_op(key, make, transposed):
        if key not in shared:
            shared[key] = (_split2(make()), transposed)
        return shared[key]

    def v_op(key, make, transposed):
        if key not in shared:
            shared[key] = (make().astype(BF16), transposed)
        return shared[key]

    new_rows = lambda ref: jnp.concatenate([ref[...], zpad], axis=0)
    out = {}

    def chain(kh):
        mk8 = (lane256 >= kh * NSA_HD) & (lane256 < (kh + 1) * NSA_HD)
        q = per_group(lambda g: jnp.where(mk8, q_ref[:, g * KV_DIM:(g + 1) * KV_DIM], 0.0))
        q_sp = _split2(q)
        lc = (_dot3s(q_sp, k_op("kc", lambda: kc_ref[...], False)[0], NT)
              + per_group(lambda g: cb_ref[kh * NSA_GROUP + g]))
        yield
        p_c = _masked_softmax_rows(lc, cend <= qpos)
        yield
        vc_sp = k_op("vc", lambda: vc_ref[...], False)[0]
        o_c = _dot3s(_split2(p_c), vc_sp)
        psum = (p_c[0:T_PAD] + p_c[T_PAD:2 * T_PAD]) + (p_c[2 * T_PAD:3 * T_PAD] + p_c[3 * T_PAD:4 * T_PAD])
        imp_t = _dot3(ovt_ref[...], jnp.concatenate([psum, zpad_t], axis=0), NT)
        yield
        cur = lax.shift_right_logical(qpos_lane, 6)
        future = sidx * SEL_BLOCK > qpos_lane
        forced = (sidx == 0) | (sidx == cur) | (sidx == cur - 1)
        score = jnp.where(future, NEG_INF, jnp.where(forced, float("inf"), imp_t))
        cnt = jnp.zeros(score.shape, F32)
        for s2 in range(n_sel_pad):
            rowv = score[s2:s2 + 1, :]
            ahead = (rowv > score) | ((rowv == score) & (s2 < sidx))
            cnt = cnt + jnp.where(ahead, 1.0, 0.0)
            if s2 % 8 == 7:
                yield
        sel_t = jnp.where(cnt < SEL_TOPN, 1.0, 0.0).astype(BF16)

        def tile_add(kt, base, extra_ok):
            kpos = base + kcol
            ok = (kpos <= qpos8) & extra_ok(kpos)
            if kt is not None:
                ok = ok & (_block_mask(sel_t, kt)[:T_PAD, :] > 0.5)
            madd = jnp.where(ok, 0.0, NEG_INF)
            return per_group(lambda g: sb_ref[kh * NSA_GROUP + g, :, base:base + LANE] + madd)

        init = lambda: (jnp.full((rows, 1), NEG_INF, F32), jnp.zeros((rows, 1), F32), jnp.zeros((rows, KV_DIM), F32))
        carry = init()
        for p0 in range(0, n, SAMPLE_GROUP):
            ps = range(p0, min(p0 + SAMPLE_GROUP, n))
            carry = yield from _online_update_wide(
                carry, q_sp,
                [k_op(("ks", p), functools.partial(lambda p_: ks_pages[p_][...], p), True) for p in ps],
                [v_op(("vs", p), functools.partial(lambda p_: vs_pages[p_][...], p), True) for p in ps],
                [tile_add(p, p * PAGE, always) for p in ps])
        carry = yield from _online_update_wide(
            carry, q_sp, [k_op("ksn", lambda: new_rows(ksn), False)], [v_op("vsn", lambda: new_rows(vsn), False)],
            [tile_add(n, past, always)])
        o_s = _attn_finish(carry)
        win_tile = lambda ref, i: functools.partial(lambda i_: ref[i_ * LANE:(i_ + 1) * LANE, :], i)
        carry = yield from _online_update_wide(
            init(), q_sp,
            [k_op(("wk", i), win_tile(wk_ref, i), False) for i in range(n_wt)]
            + [k_op("kwn", lambda: new_rows(kwn), False)],
            [v_op(("wv", i), win_tile(wv_ref, i), False) for i in range(n_wt)]
            + [v_op("vwn", lambda: new_rows(vwn), False)],
            [tile_add(None, w0 + i * LANE, in_win) for i in range(n_wt)] + [tile_add(None, past, in_win)])
        o_w = _attn_finish(carry)
        for g in range(NSA_GROUP):
            sl = slice(g * T_PAD, (g + 1) * T_PAD)
            c0 = kh * LANE + g * 3
            og = (gate_ref[:, c0:c0 + 1] * o_c[sl] + gate_ref[:, c0 + 1:c0 + 2] * o_s[sl]
                  + gate_ref[:, c0 + 2:c0 + 3] * o_w[sl])
            out[kh, g] = jnp.where(mk8, og, 0.0)

    _round_robin([chain(kh) for kh in range(NSA_KV)])
    for g in range(NSA_GROUP):
        o_refs[g][...] = (out[0, g] + out[1, g]) + (out[2, g] + out[3, g])

    rowi = lax.broadcasted_iota(jnp.int32, (LANE, KV_DIM), 0)
    for src, new_ref, dst in ((wk_ref, kwn, wko_ref), (wv_ref, vwn, wvo_ref)):
        shifted = pltpu.roll(src[...], win_buf - dec_seq, axis=0)
        tail = pltpu.roll(new_rows(new_ref), LANE - dec_seq, axis=0)
        dst[0:win_buf - LANE, :] = shifted[0:win_buf - LANE, :]
        dst[win_buf - LANE:win_buf, :] = jnp.where(rowi >= LANE - dec_seq, tail, shifted[win_buf - LANE:win_buf, :])


def _nsa_decode_attn(j, page_table, qp8, gates8, kcmp, vcmp, cache_k, cache_v, new8, win_k, win_v, sbias, scbias,
                     past, dec_seq):
    batch, n_pages = page_table.shape
    win_buf = win_k.shape[2]
    mc = kcmp.shape[1]
    n_sel = -(-(past + dec_seq) // SEL_BLOCK)
    n_sel_pad = -(-n_sel // 8) * 8
    ovt = _overlap_t(mc - 1, mc, n_sel, n_sel_pad)
    per_b = lambda a: pl.BlockSpec((None,) + a.shape[1:], lambda b, pt: (b, 0, 0))
    const = lambda a: pl.BlockSpec(a.shape, functools.partial(lambda nd, b, pt: (0,) * nd, a.ndim))
    page_specs = [pl.BlockSpec((None, None, KV_DIM, PAGE), functools.partial(lambda p, b, pt: (j, pt[b, p], 0, 0), p))
                  for p in range(n_pages)]
    win_spec = pl.BlockSpec((None, None, win_buf, KV_DIM), lambda b, pt: (j, b, 0, 0))
    in_specs = ([per_b(qp8), per_b(gates8), per_b(kcmp), per_b(vcmp)] + page_specs + page_specs
                + [per_b(a) for a in new8] + [win_spec, win_spec, const(sbias), const(scbias), const(ovt)])
    o_sds = jax.ShapeDtypeStruct((batch, T_PAD, KV_DIM), F32)
    w_sds = jax.ShapeDtypeStruct((batch, win_buf, KV_DIM), F32)
    out_shape = (o_sds,) * NSA_GROUP + (w_sds, w_sds)
    gs = pltpu.PrefetchScalarGridSpec(
        num_scalar_prefetch=1, grid=(batch,), in_specs=in_specs, out_specs=tuple(per_b(s) for s in out_shape))
    res = pl.pallas_call(
        functools.partial(_nsa_decode_kernel_t, past, dec_seq, n_pages, win_buf, n_sel_pad),
        grid_spec=gs,
        out_shape=out_shape,
        compiler_params=_cp("arbitrary"),
        name="nsa_decode_attn",
    )(page_table, qp8, gates8, kcmp, vcmp, *([cache_k] * n_pages), *([cache_v] * n_pages),
      *new8, win_k, win_v, sbias, scbias, ovt)
    return res[:NSA_GROUP], res[NSA_GROUP], res[NSA_GROUP + 1]


def _nsa_sample_layer(j, x2d, nw, caches, wins, page_table, sbias, scbias, dec_seq, past):
    batch, n_pages = page_table.shape
    n = x2d.shape[0]
    qp, kvs, sz, gates = _nsa_project_all(x2d, nw, n)
    cw_k, cw_v, w_o = nw[7:]
    m = n_pages * PAGE // CMP_STRIDE
    page_specs = lambda _: [pl.BlockSpec((None, None, KV_DIM, PAGE),
                                         functools.partial(lambda p, b, pt: (j, pt[b, p], 0, 0), p))
                            for p in range(n_pages)]
    out_idx = lambda b, pt: (b, 0, 0)
    kcmp = _compress_call([caches[0]] * n_pages, page_specs, (batch,), 1, (page_table,), cw_k, True, m, batch, out_idx,
                          transposed=True)
    vcmp = _compress_call([caches[1]] * n_pages, page_specs, (batch,), 1, (page_table,), cw_v, False, m, batch, out_idx,
                          transposed=True)
    pad8 = lambda a: _pad_to(a.reshape(batch, dec_seq, a.shape[-1]), 1, T_PAD)
    new8 = tuple(pad8(kvs[i]) for i in (2, 3, 4, 5))
    o8, win_k_new, win_v_new = _nsa_decode_attn(j, page_table, pad8(qp), pad8(gates), kcmp, vcmp, caches[2], caches[3],
                                                new8, wins[0], wins[1], sbias, scbias, past, dec_seq)
    o_parts = [o[:, :dec_seq].reshape(n, KV_DIM) for o in o8]
    x_new = _nsa_out(o_parts, sz, x2d, w_o, n)
    return x_new, kvs[:4] + (win_k_new, win_v_new)


def _nsa_out_kernel(o0, o1, o2, o3, sz_ref, x_ref, wh_ref, wl_ref, out_ref):
    o = jnp.concatenate([o0[...], o1[...], o2[...], o3[...]], axis=-1) * sz_ref[...]
    out_ref[...] = x_ref[...] + _wdot(o, wh_ref, wl_ref, 1)


def _nsa_out(o_parts, sz, x2d, w_pair, tile):
    n = x2d.shape[0]
    row_spec = pl.BlockSpec((tile, D_MODEL), lambda i: (i, 0))
    part_spec = pl.BlockSpec((tile, KV_DIM), lambda i: (i, 0))
    full = lambda a: pl.BlockSpec(a.shape, lambda i: (0,) * a.ndim)
    return pl.pallas_call(
        _nsa_out_kernel,
        grid=(n // tile,),
        in_specs=[part_spec] * 4 + [row_spec, row_spec, full(w_pair[0]), full(w_pair[1])],
        out_specs=row_spec,
        out_shape=jax.ShapeDtypeStruct((n, D_MODEL), F32),
        compiler_params=_cp("arbitrary"),
        name="nsa_out",
    )(*o_parts, sz, x2d, w_pair[0], w_pair[1])


def _nsa_weights(j, norm_w_i, nsa_w_in, nsa_q_norm, nsa_k_norm, nsa_cmp_pos, nsa_cmp_w1, nsa_cmp_w2, nsa_w_o):
    w_in = nsa_w_in[j]
    qd = NSA_HEADS * NSA_HD

    def perm_cols(w):
        return w.reshape(D_MODEL, NSA_KV, NSA_GROUP, NSA_HD).transpose(0, 2, 1, 3).reshape(D_MODEL, qd)

    w_q = perm_cols(w_in[:, :qd])
    w_kv = w_in[:, qd:qd + 6 * KV_DIM]
    w_z = perm_cols(w_in[:, qd + 6 * KV_DIM:2 * qd + 6 * KV_DIM])
    w_g = w_in[:, 2 * qd + 6 * KV_DIM:].reshape(D_MODEL, NSA_KV, 3 * NSA_GROUP)
    w_g = _pad_to(w_g, 2, LANE).reshape(D_MODEL, NSA_KV * LANE)
    w_o = nsa_w_o[j].reshape(NSA_KV, NSA_GROUP, NSA_HD, D_MODEL).transpose(1, 0, 2, 3).reshape(qd, D_MODEL)
    aux_q = jnp.zeros((8, qd), F32).at[0].set(jnp.tile(nsa_q_norm[j], NSA_HEADS))
    aux_kv = (jnp.zeros((8, KV_DIM), F32).at[0].set(jnp.tile(nsa_k_norm[j, 0], NSA_KV))
              .at[1].set(jnp.tile(nsa_k_norm[j, 1], NSA_KV)))
    cw_k = _compress_weights(nsa_cmp_pos[j, 0], nsa_cmp_w1[j, 0], nsa_cmp_w2[j, 0], nsa_k_norm[j, 2])
    cw_v = _compress_weights(nsa_cmp_pos[j, 1], nsa_cmp_w1[j, 1], nsa_cmp_w2[j, 1], nsa_k_norm[j, 2])
    return (norm_w_i[None, :], _wsplit(w_q), _wsplit(w_kv), _wsplit(w_z), _wsplit(w_g), aux_q, aux_kv,
            cw_k, cw_v, _wsplit(w_o))


def _nsa_project_all(x2d, nw, tile):
    nw_row, w_q, w_kv, w_z, w_g, aux_q, aux_kv = nw[:7]
    qp = _nsa_proj("q", x2d, nw_row, w_q, aux_q, tile)
    kvs = _nsa_proj("kv", x2d, nw_row, w_kv, aux_kv, tile)
    sz = _nsa_proj("z", x2d, nw_row, w_z, aux_q, tile)
    gates = _nsa_proj("g", x2d, nw_row, w_g, aux_q, tile)
    return qp, kvs, sz, gates


def _nsa_prompt_layer(x2d, nw, toep, cbias, batch, seq, tile):
    qp, (kc, vc, ks, vs, kw, vw), sz, gates = _nsa_project_all(x2d, nw, tile)
    cw_k, cw_v, w_o = nw[7:]
    mc = seq // CMP_STRIDE
    b3 = lambda a: a.reshape(batch, seq, KV_DIM)
    src_spec = lambda h: [pl.BlockSpec((None, seq, LANE), lambda b: (b, 0, h))]
    kcmp = _compress_call([b3(kc)], src_spec, (batch,), 0, (), cw_k, True, mc, batch, lambda b: (b, 0, 0))
    vcmp = _compress_call([b3(vc)], src_spec, (batch,), 0, (), cw_v, False, mc, batch, lambda b: (b, 0, 0))
    o_parts = _nsa_prompt_attn(qp, gates, kcmp, vcmp, b3(ks), b3(vs), b3(kw), b3(vw), toep, cbias, batch, seq)
    x_new = _nsa_out(o_parts, sz, x2d, w_o, tile)
    return x_new, (kc, vc, ks, vs, kw, vw)


PROMPT_TILE = 512
PROMPT_CHUNK = 64
SAMPLE_CHUNK = 8


def kernel(x_prompt, x_sample, cache_cmp_k, cache_cmp_v, cache_sel_k, cache_sel_v, state_win_k, state_win_v, state_wkv, state_shift, page_table, norm_w, rel_bias, rw_mu, rw_w_rkvz, rw_w0, rw_w1, rw_w2, rw_a0, rw_a1, rw_a2, rw_v0, rw_v1, rw_v2, rw_g1, rw_g2, rw_k_k, rw_k_a, rw_r_k, rw_ln_w, rw_ln_b, rw_w_o, nsa_w_in, nsa_q_norm, nsa_k_norm, nsa_cmp_pos, nsa_cmp_w1, nsa_cmp_w2, nsa_w_o):
    bp, tp, d = x_prompt.shape
    bs, ts, _ = x_sample.shape
    depth = norm_w.shape[0]
    n_pages = page_table.shape[1]
    past = n_pages * PAGE
    win_buf = state_win_k.shape[2]
    tile = min(PROMPT_TILE, tp)
    ns = bs * ts

    xp = x_prompt.reshape(bp * tp, d)
    xs = x_sample.reshape(ns, d)
    flat = lambda c: c.reshape(c.shape[0], c.shape[1], c.shape[2], KV_DIM)
    flat_t = lambda c: c.transpose(0, 1, 3, 4, 2).reshape(c.shape[0], c.shape[1], KV_DIM, c.shape[2])
    caches = tuple(flat_t(c) for c in (cache_cmp_k, cache_cmp_v, cache_sel_k, cache_sel_v))
    wins = (flat(state_win_k), flat(state_win_v))

    toep = _bias_table(rel_bias, 9, LANE, LANE, LANE, 0, 1)
    cbias = _bias_table(rel_bias, tp // LANE, LANE, tp // CMP_STRIDE, LANE, -(CMP_BLOCK - 1), CMP_STRIDE)
    sbias = _bias_table(rel_bias, 1, T_PAD, past + LANE, 0, past, 1)[0]
    scbias = _bias_table(rel_bias, 1, T_PAD, past // CMP_STRIDE, 0, past - (CMP_BLOCK - 1), CMP_STRIDE)[0]

    to_tm = lambda a: a.reshape(bs, ts, d).transpose(1, 0, 2).reshape(ns, d)
    from_tm = lambda a: a.reshape(ts, bs, d).transpose(1, 0, 2).reshape(ns, d)
    p_to_seq = lambda a: a.reshape(bp, tp, d)
    p_from_seq = lambda a: a.reshape(bp * tp, d)
    s_to_seq = lambda a: _pad_to(a.reshape(ts, bs, d).transpose(1, 0, 2), 1, SAMPLE_CHUNK)
    s_from_seq = lambda a: a[:, :ts].transpose(1, 0, 2).reshape(ns, d)

    vf_p = vf_s = None
    p_wkv, p_shift, s_wkv, s_shift = [], [], [], []
    p_kv = [[] for _ in range(6)]
    s_kv = [[] for _ in range(6)]
    for i in range(depth):
        j = i // 2
        if i % 2 == 0:
            lw = _rwkv_weights(j, norm_w[i], rw_mu, rw_w_rkvz, rw_w0, rw_w1, rw_w2, rw_a0, rw_a1, rw_a2, rw_v0,
                               rw_v1, rw_v2, rw_g1, rw_g2, rw_k_k, rw_k_a, rw_r_k, rw_ln_w, rw_ln_b, rw_w_o)
            xp, sh, sblk, vf_p = _rwkv_layer(xp, jnp.zeros((bp, 1, d), F32), jnp.zeros((bp, RW_PAIRS, LANE, LANE), F32),
                                             vf_p, lw, tile, p_to_seq, p_from_seq, PROMPT_CHUNK)
            p_wkv.append(_unblock_state(sblk))
            p_shift.append(sh[:, 0])
            xs_tm, sh, sblk, vf_s = _rwkv_layer(to_tm(xs), state_shift[j][None], _blockdiag_state(state_wkv[j]),
                                                vf_s, lw, ns, s_to_seq, s_from_seq, SAMPLE_CHUNK)
            xs = from_tm(xs_tm)
            s_wkv.append(_unblock_state(sblk))
            s_shift.append(sh[0])
        else:
            nw = _nsa_weights(j, norm_w[i], nsa_w_in, nsa_q_norm, nsa_k_norm, nsa_cmp_pos, nsa_cmp_w1, nsa_cmp_w2,
                              nsa_w_o)
            xp, newp = _nsa_prompt_layer(xp, nw, toep, cbias, bp, tp, tile)
            xs, news = _nsa_sample_layer(j, xs, nw, caches, wins, page_table, sbias, scbias, ts, past)
            wb = min(WINDOW, tp)
            for m_ in range(4):
                p_kv[m_].append(newp[m_].reshape(bp, tp, NSA_KV, NSA_HD))
                s_kv[m_].append(news[m_].reshape(bs, ts, NSA_KV, NSA_HD))
            for m_ in (4, 5):
                p_kv[m_].append(newp[m_].reshape(bp, tp, NSA_KV, NSA_HD)[:, tp - wb:])
                s_kv[m_].append(news[m_].reshape(bs, win_buf, NSA_KV, NSA_HD))
    st = jnp.stack
    return (xp.reshape(bp, tp, d), xs.reshape(bs, ts, d), st(p_wkv), st(p_shift),
            st(p_kv[0]), st(p_kv[1]), st(p_kv[2]), st(p_kv[3]), st(p_kv[4]), st(p_kv[5]),
            st(s_wkv), st(s_shift),
            st(s_kv[0]), st(s_kv[1]), st(s_kv[2]), st(s_kv[3]), st(s_kv[4]), st(s_kv[5]))
```
